```python
import math
import jax, jax.numpy as jnp
from jax import lax
import numpy as np

D_MODEL = 2048
BATCH = 1
SEQ = 16384
DEPTH = 1
DEC_BATCH = 4
DEC_SEQ = 8192
PAST_LEN = 128

A_HEADS = 8
A_HEAD_DIM = 64
A_VDIM = 2 * A_HEAD_DIM
A_WIDTH = A_HEADS * A_VDIM
A_QK_COLS = A_HEADS * 2 * A_HEAD_DIM
ROPE_THETA = 10000.0
Q_BLOCK = 128
R_HEAD = 64
R_WIDTH = 1024
R_HEADS = R_WIDTH // R_HEAD
DECAY_RANK = 64
AAA_RANK = 64
GATE_RANK = 160
R_COLS = 3 * R_WIDTH + 2 * DECAY_RANK + 2 * AAA_RANK + GATE_RANK
LNX_EPS = 64e-5
N_MEM = 256
M_HEADS = 4
M_HEAD_DIM = 256
M_WIDTH = M_HEADS * M_HEAD_DIM
N_BRANCH = 3
IN_OFFSETS = (A_QK_COLS, 2 * A_QK_COLS, 2 * A_QK_COLS + A_WIDTH,
              2 * A_QK_COLS + A_WIDTH + R_COLS,
              2 * A_QK_COLS + A_WIDTH + R_COLS + M_WIDTH)
IN_COLS = 2 * A_QK_COLS + A_WIDTH + R_COLS + M_WIDTH + N_BRANCH * D_MODEL
D_FF = 5632
CONV_W = 3
EPS = 1e-6

kernel_name = 'hybrid_diffattn_rwkv7_memxattn_convglu_encoder'


def rmsnorm(x, g, eps=EPS):
    xf = x.astype(jnp.float32)
    y = xf * lax.rsqrt(jnp.mean(xf * xf, axis=-1, keepdims=True) + eps)
    return (y * g.astype(jnp.float32)).astype(x.dtype)


def shift_prev(u):
    pad = [(0, 0)] * u.ndim
    pad[1] = (1, 0)
    return jnp.pad(u[:, :-1], pad)


def shift_next(u):
    pad = [(0, 0)] * u.ndim
    pad[1] = (0, 1)
    return jnp.pad(u[:, 1:], pad)


def rope(x, pos):
    half = x.shape[-1] // 2
    inv = jnp.power(ROPE_THETA, -jnp.arange(half, dtype=jnp.float32) / half)
    ang = pos[:, None] * inv[None, :]
    bshape = (1, x.shape[1]) + (1,) * (x.ndim - 3) + (half,)
    cos = jnp.cos(ang).reshape(bshape)
    sin = jnp.sin(ang).reshape(bshape)
    xf = x.astype(jnp.float32)
    x1, x2 = xf[..., :half], xf[..., half:]
    return jnp.concatenate([x1 * cos - x2 * sin, x2 * cos + x1 * sin], axis=-1).astype(x.dtype)


def diff_attention(q, k, v, lam, lam_init, sub_g):
    B, T = q.shape[0], q.shape[1]
    nb = T // Q_BLOCK
    scale = A_HEAD_DIM ** -0.5
    qb = q.reshape(B, nb, Q_BLOCK, A_HEADS, 2, A_HEAD_DIM).transpose(1, 0, 2, 3, 4, 5)

    def block(qi):
        s = jnp.einsum('bqhcd,bkhcd->bhcqk', qi, k, preferred_element_type=jnp.float32) * scale
        p = jax.nn.softmax(s, axis=-1)
        a = p[:, :, 0] - lam * p[:, :, 1]
        return jnp.einsum('bhqk,bkhe->bqhe', a.astype(v.dtype), v)

    o = lax.map(block, qb)
    o = o.transpose(1, 0, 2, 3, 4).reshape(B, T, A_HEADS, A_VDIM)
    o = rmsnorm(o, sub_g) * (1.0 - lam_init)
    return o.reshape(B, T, A_WIDTH)


def rwkv7_bidir(z, mu_prev, mu_next, decay_w0, decay_w2, aaa_a0, aaa_a2, gate_g2,
                k_k, k_a, r_k, lnx_w, lnx_b):
    B, T = z.shape[0], z.shape[1]
    H, N = R_HEADS, R_HEAD
    f32 = jnp.float32
    z = z + mu_prev * (shift_prev(z) - z) + mu_next * (shift_next(z) - z)
    r, k, v, zw, za, zg = jnp.split(
        z, (R_WIDTH, 2 * R_WIDTH, 3 * R_WIDTH, 3 * R_WIDTH + 2 * DECAY_RANK,
            3 * R_WIDTH + 2 * DECAY_RANK + 2 * AAA_RANK), axis=-1)
    zw = zw.reshape(B, T, 2, DECAY_RANK)
    za = za.reshape(B, T, 2, AAA_RANK)
    w = -jax.nn.softplus(-(decay_w0 + jnp.einsum('btdr,drc->btdc', jnp.tanh(zw), decay_w2))) - 0.5
    decay = jnp.exp(-jnp.exp(w.astype(f32)))
    a = jax.nn.sigmoid(aaa_a0 + jnp.einsum('btdr,drc->btdc', za, aaa_a2)).astype(f32)
    g = jax.nn.sigmoid(zg) @ gate_g2
    kk = (k * k_k).astype(f32).reshape(B, T, H, N)
    kk = kk / jnp.maximum(jnp.sqrt(jnp.sum(kk * kk, axis=-1, keepdims=True)), 1e-12)
    kk = kk.reshape(B, T, R_WIDTH)
    kd = (k[:, :, None, :] * (1.0 + (a - 1.0) * k_a)).astype(f32)

    def both(u):
        return jnp.stack([u, u], axis=2).astype(f32)

    def dirs(u):
        u = u.reshape(B, T, 2, H, N)
        u = jnp.stack([u[:, :, 0], u[:, ::-1, 1]], axis=0)
        return u.transpose(2, 0, 1, 3, 4)

    xs = (dirs(both(r)), dirs(decay), dirs(kd), dirs(both(v)), dirs(both(kk)), dirs(a))

    def step(S, inp):
        r_t, w_t, k_t, v_t, kk_t, a_t = inp
        sa = jnp.einsum('dbhij,dbhj->dbhi', S, -kk_t)
        S = (S * w_t[..., None, :] + sa[..., None] * (kk_t * a_t)[..., None, :]
             + v_t[..., None] * k_t[..., None, :])
        y = jnp.einsum('dbhij,dbhj->dbhi', S, r_t)
        return S, y

    S0 = jnp.zeros((2, B, H, N, N), f32)
    _, ys = lax.scan(step, S0, xs)
    y = (ys[:, 0] + ys[::-1, 1]).transpose(1, 0, 2, 3)
    mu = jnp.mean(y, axis=-1, keepdims=True)
    var = jnp.mean((y - mu) ** 2, axis=-1, keepdims=True)
    yn = ((y - mu) * lax.rsqrt(var + LNX_EPS)).reshape(B, T, R_WIDTH)
    yn = yn * lnx_w.astype(f32) + lnx_b.astype(f32)
    rh = r.astype(f32).reshape(B, T, 1, H, N)
    kdh = kd.reshape(B, T, 2, H, N)
    bonus = jnp.sum(rh * kdh * r_k.astype(f32), axis=(2, 4))[..., None] * v.astype(f32).reshape(B, T, H, N)
    out = (yn + bonus.reshape(B, T, R_WIDTH)) * g.astype(f32)
    return out.astype(z.dtype)


def memory_attention(q, mem, g_memnorm, w_mem_kv, mq_norm, mk_norm):
    B, T = q.shape[0], q.shape[1]
    kv = rmsnorm(mem, g_memnorm) @ w_mem_kv
    km, vm = jnp.split(kv, (M_WIDTH,), axis=-1)
    qh = rmsnorm(q.reshape(B, T, M_HEADS, M_HEAD_DIM), mq_norm)
    kh = rmsnorm(km.reshape(B, N_MEM, M_HEADS, M_HEAD_DIM), mk_norm)
    vh = vm.reshape(B, N_MEM, M_HEADS, M_HEAD_DIM)
    s = jnp.einsum('bqhd,bkhd->bhqk', qh, kh, preferred_element_type=jnp.float32) * (M_HEAD_DIM ** -0.5)
    p = jax.nn.softmax(s, axis=-1)
    o = jnp.einsum('bhqk,bkhd->bqhd', p.astype(vh.dtype), vh)
    return o.reshape(B, T, M_WIDTH)


def conv_glu_ffn(h, w_up, conv_w, conv_b, w_down):
    u = h @ w_up
    u = conv_w[0] * shift_prev(u) + conv_w[1] * u + conv_w[2] * shift_next(u) + conv_b
    gt, val = jnp.split(u, (D_FF,), axis=-1)
    return (jax.nn.silu(gt) * val) @ w_down


def encoder_layer(x, mem, p, layer_idx):
    B, T = x.shape[0], x.shape[1]
    lam_init = 0.8 - 0.6 * math.exp(-0.3 * layer_idx)
    h = rmsnorm(x, p['g_mix'])
    z = h @ p['w_in']
    zq, zk, zv, zr, zm, zg = jnp.split(z, IN_OFFSETS, axis=-1)
    pos = jnp.arange(T, dtype=jnp.float32)
    q = rope(rmsnorm(zq.reshape(B, T, A_HEADS, 2, A_HEAD_DIM), p['q_norm']), pos)
    k = rope(rmsnorm(zk.reshape(B, T, A_HEADS, 2, A_HEAD_DIM), p['k_norm']), pos)
    v = zv.reshape(B, T, A_HEADS, A_VDIM)
    f32 = jnp.float32
    lam = (jnp.exp(jnp.sum(p['lambda_q1'].astype(f32) * p['lambda_k1'].astype(f32)))
           - jnp.exp(jnp.sum(p['lambda_q2'].astype(f32) * p['lambda_k2'].astype(f32))) + lam_init)
    o_a = diff_attention(q, k, v, lam, lam_init, p['attn_subln'])
    o_r = rwkv7_bidir(zr, p['mu_prev'], p['mu_next'], p['decay_w0'], p['decay_w2'],
                      p['aaa_a0'], p['aaa_a2'], p['gate_g2'], p['k_k'], p['k_a'],
                      p['r_k'], p['lnx_w'], p['lnx_b'])
    o_m = memory_attention(zm, mem, p['g_memnorm'], p['w_mem_kv'], p['mq_norm'], p['mk_norm'])
    g_a, g_r, g_m = jnp.split(jax.nn.sigmoid(zg), N_BRANCH, axis=-1)
    merged = g_a * (o_a @ p['p_attn']) + g_r * (o_r @ p['p_rwkv']) + g_m * (o_m @ p['p_mem'])
    x = x + merged @ p['w_o']
    x = x + conv_glu_ffn(rmsnorm(x, p['g_ffn']), p['w_up'], p['conv_w'], p['conv_b'], p['w_down'])
    return x


def run_trunk(x, mem, params):
    for l in range(DEPTH):
        layer_params = {name: arr[l] for name, arr in params.items()}
        x = encoder_layer(x, mem, layer_params, l)
    return x


def setup_inputs(seed: int = 0) -> dict:
    key = jax.random.key(seed)
    ks = iter(jax.random.split(key, 48))
    f32 = jnp.float32
    L = DEPTH

    def nrm(shape, scale):
        return scale * jax.random.normal(next(ks), shape, f32)

    def gain(shape):
        return 1.0 + nrm(shape, 0.02)

    def unif(shape, lo, hi):
        return jax.random.uniform(next(ks), shape, f32, lo, hi)

    return {
        'x_prompt': nrm((BATCH, SEQ, D_MODEL), 1.0),
        'x_sample': nrm((DEC_BATCH, DEC_SEQ, D_MODEL), 1.0),
        'mem_prompt': nrm((BATCH, N_MEM, D_MODEL), 1.0),
        'mem_sample': nrm((DEC_BATCH, N_MEM, D_MODEL), 1.0),
        'g_mix': gain((L, D_MODEL)),
        'w_in': nrm((L, D_MODEL, IN_COLS), D_MODEL ** -0.5),
        'q_norm': gain((L, A_HEAD_DIM)),
        'k_norm': gain((L, A_HEAD_DIM)),
        'lambda_q1': nrm((L, A_HEAD_DIM), 0.1),
        'lambda_k1': nrm((L, A_HEAD_DIM), 0.1),
        'lambda_q2': nrm((L, A_HEAD_DIM), 0.1),
        'lambda_k2': nrm((L, A_HEAD_DIM), 0.1),
        'attn_subln': gain((L, A_VDIM)),
        'mu_prev': unif((L, R_COLS), 0.0, 0.4),
        'mu_next': unif((L, R_COLS), 0.0, 0.4),
        'decay_w0': nrm((L, 2, R_WIDTH), 0.5),
        'decay_w2': nrm((L, 2, DECAY_RANK, R_WIDTH), 0.1),
        'aaa_a0': nrm((L, 2, R_WIDTH), 0.1),
        'aaa_a2': nrm((L, 2, AAA_RANK, R_WIDTH), 0.5 * AAA_RANK ** -0.5),
        'gate_g2': nrm((L, GATE_RANK, R_WIDTH), GATE_RANK ** -0.5),
        'k_k': 0.85 + nrm((L, R_WIDTH), 0.05),
        'k_a': 1.0 + nrm((L, R_WIDTH), 0.05),
        'r_k': nrm((L, R_HEADS, R_HEAD), 0.1),
        'lnx_w': gain((L, R_WIDTH)),
        'lnx_b': nrm((L, R_WIDTH), 0.01),
        'g_memnorm': gain((L, D_MODEL)),
        'w_mem_kv': nrm((L, D_MODEL, 2 * M_WIDTH), D_MODEL ** -0.5),
        'mq_norm': gain((L, M_HEAD_DIM)),
        'mk_norm': gain((L, M_HEAD_DIM)),
        'p_attn': nrm((L, A_WIDTH, D_MODEL), A_WIDTH ** -0.5),
        'p_rwkv': nrm((L, R_WIDTH, D_MODEL), R_WIDTH ** -0.5),
        'p_mem': nrm((L, M_WIDTH, D_MODEL), M_WIDTH ** -0.5),
        'w_o': nrm((L, D_MODEL, D_MODEL), D_MODEL ** -0.5),
        'g_ffn': gain((L, D_MODEL)),
        'w_up': nrm((L, D_MODEL, 2 * D_FF), D_MODEL ** -0.5),
        'conv_w': jnp.array([0.0, 1.0, 0.0], f32)[None, :, None] + nrm((L, CONV_W, 2 * D_FF), 0.2),
        'conv_b': nrm((L, 2 * D_FF), 0.01),
        'w_down': nrm((L, D_FF, D_MODEL), D_FF ** -0.5),
    }


def reference(x_prompt, x_sample, mem_prompt, mem_sample, g_mix, w_in, q_norm, k_norm,
              lambda_q1, lambda_k1, lambda_q2, lambda_k2, attn_subln, mu_prev, mu_next,
              decay_w0, decay_w2, aaa_a0, aaa_a2, gate_g2, k_k, k_a, r_k, lnx_w, lnx_b,
              g_memnorm, w_mem_kv, mq_norm, mk_norm, p_attn, p_rwkv, p_mem, w_o,
              g_ffn, w_up, conv_w, conv_b, w_down):
    params = dict(g_mix=g_mix, w_in=w_in, q_norm=q_norm, k_norm=k_norm,
                  lambda_q1=lambda_q1, lambda_k1=lambda_k1, lambda_q2=lambda_q2,
                  lambda_k2=lambda_k2, attn_subln=attn_subln, mu_prev=mu_prev,
                  mu_next=mu_next, decay_w0=decay_w0, decay_w2=decay_w2, aaa_a0=aaa_a0,
                  aaa_a2=aaa_a2, gate_g2=gate_g2, k_k=k_k, k_a=k_a, r_k=r_k,
                  lnx_w=lnx_w, lnx_b=lnx_b, g_memnorm=g_memnorm, w_mem_kv=w_mem_kv,
                  mq_norm=mq_norm, mk_norm=mk_norm, p_attn=p_attn, p_rwkv=p_rwkv,
                  p_mem=p_mem, w_o=w_o, g_ffn=g_ffn, w_up=w_up, conv_w=conv_w,
                  conv_b=conv_b, w_down=w_down)
    y_prompt = run_trunk(x_prompt, mem_prompt, params)
    y_sample = run_trunk(x_sample, mem_sample, params)
    return (y_prompt, y_sample)
```

```python
import functools
import math

import jax
import jax.numpy as jnp
from jax import lax
from jax.experimental import pallas as pl
from jax.experimental.pallas import tpu as pltpu

F32 = jnp.float32
BF16 = jnp.bfloat16

D_MODEL = 2048
A_HEADS = 8
A_HEAD_DIM = 64
A_WIDTH = 1024
ROPE_THETA = 10000.0
R_HEAD = 64
R_WIDTH = 1024
R_COLS = 3488
R_COLS_PAD = 3584
GATE_RANK = 160
GATE_RANK_PAD = 256
LNX_EPS = 64e-5
N_MEM = 256
M_HEADS = 4
M_HEAD_DIM = 256
M_WIDTH = 1024
D_FF = 5632
EPS = 1e-6
LANES = 128
CHUNK = 64

NN = (((1,), (0,)), ((), ()))
NT = (((1,), (1,)), ((), ()))
TN = (((0,), (0,)), ((), ()))


def _params(sem, vmem_mb):
    return pltpu.CompilerParams(dimension_semantics=sem, vmem_limit_bytes=vmem_mb << 20)


def _dot(a, b, dims=NN):
    return lax.dot_general(a, b, dims, preferred_element_type=F32)


def _rmsnorm_kernel(x_ref, g_ref, o_ref):
    x = x_ref[...]
    ms = jnp.mean(x * x, axis=-1, keepdims=True)
    o_ref[...] = (x * lax.rsqrt(ms + EPS) * g_ref[...]).astype(o_ref.dtype)


def rmsnorm_rows(x2, g, bm=512):
    m, d = x2.shape
    bm = min(bm, m)
    return pl.pallas_call(
        _rmsnorm_kernel,
        out_shape=jax.ShapeDtypeStruct((m, d), BF16),
        grid=(m // bm,),
        in_specs=[pl.BlockSpec((bm, d), lambda i: (i, 0)),
                  pl.BlockSpec((1, d), lambda i: (0, 0))],
        out_specs=pl.BlockSpec((bm, d), lambda i: (i, 0)),
        compiler_params=_params(("parallel",), 40),
        name="rmsnorm_rows",
    )(x2, g.reshape(1, d))


def _mm_kernel(a_ref, b_ref, o_ref, *, act):
    acc = _dot(a_ref[...], b_ref[...])
    if act == "sigmoid":
        acc = jax.nn.sigmoid(acc)
    o_ref[...] = acc.astype(o_ref.dtype)


def matmul(a, b, out_dtype, act=None, bm=1024, bn=512):
    m, k = a.shape
    n = b.shape[1]
    bm, bn = min(bm, m), min(bn, n)
    return pl.pallas_call(
        functools.partial(_mm_kernel, act=act),
        out_shape=jax.ShapeDtypeStruct((m, n), out_dtype),
        grid=(m // bm, n // bn),
        in_specs=[pl.BlockSpec((bm, k), lambda i, j: (i, 0)),
                  pl.BlockSpec((k, bn), lambda i, j: (0, j))],
        out_specs=pl.BlockSpec((bm, bn), lambda i, j: (i, j)),
        compiler_params=_params(("parallel", "arbitrary"), 48),
        name="matmul",
    )(a, b)


def _mm_res_kernel(a_ref, b_ref, r_ref, o_ref):
    o_ref[...] = r_ref[...] + _dot(a_ref[...], b_ref[...])


def matmul_residual(a, b, res, bm=1024, bn=512):
    m, k = a.shape
    n = b.shape[1]
    bm, bn = min(bm, m), min(bn, n)
    return pl.pallas_call(
        _mm_res_kernel,
        out_shape=jax.ShapeDtypeStruct((m, n), F32),
        grid=(m // bm, n // bn),
        in_specs=[pl.BlockSpec((bm, k), lambda i, j: (i, 0)),
                  pl.BlockSpec((k, bn), lambda i, j: (0, j)),
                  pl.BlockSpec((bm, bn), lambda i, j: (i, j))],
        out_specs=pl.BlockSpec((bm, bn), lambda i, j: (i, j)),
        compiler_params=_params(("parallel", "arbitrary"), 56),
        name="matmul_residual",
    )(a, b, res)


def _group_ones(group):
    r = lax.broadcasted_iota(jnp.int32, (2 * LANES, LANES), 0)
    c = lax.broadcasted_iota(jnp.int32, (2 * LANES, LANES), 1)
    return jnp.where((r % LANES) // group == c // group, 1.0, 0.0).astype(BF16)


def _group_sum(x, ones):
    hi = x.astype(BF16)
    lo = (x - hi.astype(F32)).astype(BF16)
    outs = []
    for c in range(x.shape[1] // LANES):
        sl = slice(LANES * c, LANES * (c + 1))
        outs.append(_dot(jnp.concatenate([hi[:, sl], lo[:, sl]], axis=1), ones))
    return outs[0] if len(outs) == 1 else jnp.concatenate(outs, axis=1)


def _tile_lanes(x, reps):
    return x if reps == 1 else jnp.concatenate([x] * reps, axis=1)


def _shift_rows(u, prev_row, next_row):
    n = u.shape[0]
    row = lax.broadcasted_iota(jnp.int32, u.shape, 0)
    up = jnp.where(row == 0, prev_row, pltpu.roll(u, 1, 0))
    dn = jnp.where(row == n - 1, next_row, pltpu.roll(u, n - 1, 0))
    return up, dn


def _qk_prep_kernel(x_ref, cos_ref, sin_ref, gq_ref, gk_ref, o_ref):
    c = pl.program_id(2)
    x = x_ref[0].astype(F32)
    reps = x.shape[1] // LANES
    ss = _group_sum(x * x, _group_ones(A_HEAD_DIM))
    g = jnp.where(c == 0, gq_ref[...], gk_ref[...])
    xn = x * lax.rsqrt(ss * (1.0 / A_HEAD_DIM) + EPS) * _tile_lanes(g, reps)
    lane = lax.broadcasted_iota(jnp.int32, x.shape, 1)
    half = A_HEAD_DIM // 2
    rot = jnp.where(lane % A_HEAD_DIM < half,
                    -pltpu.roll(xn, x.shape[1] - half, 1), pltpu.roll(xn, half, 1))
    y = xn * _tile_lanes(cos_ref[...], reps) + rot * _tile_lanes(sin_ref[...], reps)
    scale = jnp.where(c == 0, A_HEAD_DIM ** -0.5, 1.0)
    o_ref[0] = (y * scale).astype(o_ref.dtype)


def qk_prep(zqkv, cos, sin, q_norm, k_norm, bt=512):
    b, t, _ = zqkv.shape
    bt = min(bt, t)
    gq = jnp.tile(q_norm.reshape(1, A_HEAD_DIM), (1, LANES // A_HEAD_DIM))
    gk = jnp.tile(k_norm.reshape(1, A_HEAD_DIM), (1, LANES // A_HEAD_DIM))
    return pl.pallas_call(
        _qk_prep_kernel,
        out_shape=jax.ShapeDtypeStruct((b, t, 2 * A_WIDTH), BF16),
        grid=(b, t // bt, 2),
        in_specs=[pl.BlockSpec((1, bt, A_WIDTH), lambda i, j, c: (i, j, c)),
                  pl.BlockSpec((bt, LANES), lambda i, j, c: (j, 0)),
                  pl.BlockSpec((bt, LANES), lambda i, j, c: (j, 0)),
                  pl.BlockSpec((1, LANES), lambda i, j, c: (0, 0)),
                  pl.BlockSpec((1, LANES), lambda i, j, c: (0, 0))],
        out_specs=pl.BlockSpec((1, bt, A_WIDTH), lambda i, j, c: (i, j, c)),
        compiler_params=_params(("parallel", "parallel", "arbitrary"), 40),
        name="qk_prep",
    )(zqkv, cos, sin, gq, gk)


def _dattn_kernel(lq1_ref, lk1_ref, lq2_ref, lk2_ref, g_ref, q_ref, k_ref, v_ref, o_ref,
                  m_ref, l_ref, acc_ref, *, bk, lam_init):
    bq = q_ref.shape[1]
    nk = k_ref.shape[1] // bk
    q = q_ref[0]
    lane = lax.broadcasted_iota(jnp.int32, q.shape, 1)
    zero = jnp.zeros_like(q)
    first = lane < A_HEAD_DIM
    qq = jnp.concatenate([jnp.where(first, q, zero), jnp.where(first, zero, q)], axis=0)
    m_ref[...] = jnp.full(m_ref.shape, -jnp.inf, F32)
    l_ref[...] = jnp.zeros(l_ref.shape, F32)
    acc_ref[...] = jnp.zeros(acc_ref.shape, F32)

    def body(j, carry):
        start = pl.multiple_of(j * bk, bk)
        ks = k_ref[0, pl.ds(start, bk), :]
        vs = v_ref[0, pl.ds(start, bk), :]
        s = _dot(qq, ks, NT)
        m_prev = m_ref[...]
        m_new = jnp.maximum(m_prev, jnp.max(s, axis=-1, keepdims=True))
        alpha = jnp.exp(m_prev - m_new)
        p = jnp.exp(s - _tile_lanes(m_new, bk // LANES))
        l_ref[...] = alpha * l_ref[...] + jnp.sum(p, axis=-1, keepdims=True)
        acc_ref[...] = alpha * acc_ref[...] + _dot(p.astype(BF16), vs)
        m_ref[...] = m_new
        return carry

    lax.fori_loop(0, nk, body, 0)
    lam = (jnp.exp(jnp.sum(lq1_ref[...] * lk1_ref[...], keepdims=True))
           - jnp.exp(jnp.sum(lq2_ref[...] * lk2_ref[...], keepdims=True)) + lam_init)
    o12 = acc_ref[...] / l_ref[...]
    o = o12[:bq] - lam * o12[bq:]
    ms = jnp.mean(o * o, axis=-1, keepdims=True)
    y = (o * lax.rsqrt(ms + EPS) * g_ref[...]) * (1.0 - lam_init)
    o_ref[0] = y.astype(o_ref.dtype)


def diff_attention(qk, zqkv, lq1, lk1, lq2, lk2, sub_g, lam_init, bq=512, bk=512):
    b, t, _ = qk.shape
    bq, bk = min(bq, t), min(bk, t)
    vec = lambda a: a.reshape(1, -1)
    small = lambda n: pl.BlockSpec((1, n), lambda i, h, j: (0, 0))
    return pl.pallas_call(
        functools.partial(_dattn_kernel, bk=bk, lam_init=lam_init),
        out_shape=jax.ShapeDtypeStruct((b, t, A_WIDTH), BF16),
        grid=(b, A_HEADS, t // bq),
        in_specs=[small(A_HEAD_DIM), small(A_HEAD_DIM), small(A_HEAD_DIM), small(A_HEAD_DIM),
                  small(LANES),
                  pl.BlockSpec((1, bq, LANES), lambda i, h, j: (i, j, h)),
                  pl.BlockSpec((1, t, LANES), lambda i, h, j: (i, 0, A_HEADS + h)),
                  pl.BlockSpec((1, t, LANES), lambda i, h, j: (i, 0, 2 * A_HEADS + h))],
        out_specs=pl.BlockSpec((1, bq, LANES), lambda i, h, j: (i, j, h)),
        scratch_shapes=[pltpu.VMEM((2 * bq, LANES), F32), pltpu.VMEM((2 * bq, LANES), F32),
                        pltpu.VMEM((2 * bq, LANES), F32)],
        compiler_params=_params(("parallel", "parallel", "arbitrary"), 48),
        name="diff_attention",
    )(vec(lq1), vec(lk1), vec(lq2), vec(lk2), vec(sub_g), qk, qk, zqkv)


def _mem_attn_kernel(q_ref, kv_ref, gq_ref, gk_ref, o_ref):
    outs = []
    for h in range(M_HEADS):
        sl = slice(h * M_HEAD_DIM, (h + 1) * M_HEAD_DIM)
        qh = q_ref[0, :, sl].astype(F32)
        qn = qh * lax.rsqrt(jnp.mean(qh * qh, axis=-1, keepdims=True) + EPS) * gq_ref[...]
        kh = kv_ref[0, :, sl]
        kn = kh * lax.rsqrt(jnp.mean(kh * kh, axis=-1, keepdims=True) + EPS) * gk_ref[...]
        vh = kv_ref[0, :, M_WIDTH + h * M_HEAD_DIM:M_WIDTH + (h + 1) * M_HEAD_DIM]
        s = _dot((qn * M_HEAD_DIM ** -0.5).astype(BF16), kn.astype(BF16), NT)
        p = jnp.exp(s - jnp.max(s, axis=-1, keepdims=True))
        o = _dot(p.astype(BF16), vh.astype(BF16)) / jnp.sum(p, axis=-1, keepdims=True)
        outs.append(o)
    o_ref[0] = jnp.concatenate(outs, axis=1).astype(o_ref.dtype)


def memory_attention(zm, kv, mq_norm, mk_norm, bt=512):
    b, t, _ = zm.shape
    bt = min(bt, t)
    return pl.pallas_call(
        _mem_attn_kernel,
        out_shape=jax.ShapeDtypeStruct((b, t, M_WIDTH), BF16),
        grid=(b, t // bt),
        in_specs=[pl.BlockSpec((1, bt, M_WIDTH), lambda i, j: (i, j, 0)),
                  pl.BlockSpec((1, N_MEM, 2 * M_WIDTH), lambda i, j: (i, 0, 0)),
                  pl.BlockSpec((1, M_HEAD_DIM), lambda i, j: (0, 0)),
                  pl.BlockSpec((1, M_HEAD_DIM), lambda i, j: (0, 0))],
        out_specs=pl.BlockSpec((1, bt, M_WIDTH), lambda i, j: (i, j, 0)),
        compiler_params=_params(("parallel", "arbitrary"), 40),
        name="memory_attention",
    )(zm, kv, mq_norm.reshape(1, -1), mk_norm.reshape(1, -1))


def _rwkv_prep_kernel(z_ref, zp_ref, zn_ref, mup_ref, mun_ref, w0_ref, w2_ref, a0_ref, a2_ref, g2_ref,
                      kk_w_ref, ka_ref, rk_ref,
                      r_ref, v_ref, kk_ref, lw_ref, kd_ref, bb_ref, g_ref, bonus_ref):
    j = pl.program_id(1)
    nt = pl.num_programs(1)
    z = z_ref[0]
    prev_row = jnp.where(j > 0, zp_ref[0, 7:8, :], 0.0)
    next_row = jnp.where(j < nt - 1, zn_ref[0, 0:1, :], 0.0)
    zp, zn = _shift_rows(z, prev_row, next_row)
    z = z + mup_ref[...] * (zp - z) + mun_ref[...] * (zn - z)
    w = R_WIDTH
    r, k, v = z[:, 0:w], z[:, w:2 * w], z[:, 2 * w:3 * w]
    zw = z[:, 3 * w:3 * w + LANES]
    za = z[:, 3 * w + LANES:3 * w + 2 * LANES]
    zg = z[:, 3 * w + 2 * LANES:]
    wlin = w0_ref[...] + _dot(jnp.tanh(zw).astype(BF16), w2_ref[...])
    neg = -wlin
    softplus = jnp.maximum(neg, 0.0) + jnp.log(1.0 + jnp.exp(-jnp.abs(neg)))
    lw = -jnp.exp(-softplus - 0.5)
    a = jax.nn.sigmoid(a0_ref[...] + _dot(za.astype(BF16), a2_ref[...]))
    g = _dot(jax.nn.sigmoid(zg).astype(BF16), g2_ref[...])
    ones = _group_ones(R_HEAD)
    kk = k * kk_w_ref[...]
    norm = jnp.sqrt(_group_sum(kk * kk, ones))
    kk = kk / jnp.maximum(norm, 1e-12)
    ka = ka_ref[...]
    kd0 = k * (1.0 + (a[:, :w] - 1.0) * ka)
    kd1 = k * (1.0 + (a[:, w:] - 1.0) * ka)
    bonus = _group_sum(r * (kd0 + kd1) * rk_ref[...], ones) * v
    r_ref[0] = r
    v_ref[0] = v
    kk_ref[0] = kk
    lw_ref[0, 0] = lw[:, :w]
    lw_ref[0, 1] = lw[:, w:]
    kd_ref[0, 0] = kd0
    kd_ref[0, 1] = kd1
    bb_ref[0, 0] = kk * a[:, :w]
    bb_ref[0, 1] = kk * a[:, w:]
    g_ref[0] = g
    bonus_ref[0] = bonus


def rwkv_prep(zr, mu_prev, mu_next, decay_w0, decay_w2, aaa_a0, aaa_a2, gate_g2, k_k, k_a, r_k, bt=256):
    b, t, cols = zr.shape
    bt = min(bt, t)
    w = R_WIDTH
    pad = cols - R_COLS
    mup = jnp.pad(mu_prev, (0, pad)).reshape(1, cols)
    mun = jnp.pad(mu_next, (0, pad)).reshape(1, cols)

    def blockdiag(m):
        z = jnp.zeros_like(m[0])
        return jnp.concatenate([jnp.concatenate([m[0], z], axis=1),
                                jnp.concatenate([z, m[1]], axis=1)], axis=0).astype(BF16)

    g2 = jnp.pad(gate_g2, ((0, GATE_RANK_PAD - GATE_RANK), (0, 0))).astype(BF16)
    row = lambda n: pl.BlockSpec((1, n), lambda i, j: (0, 0))
    full = lambda r_, c_: pl.BlockSpec((r_, c_), lambda i, j: (0, 0))
    hb = bt // 8
    nh = t // 8
    tok = pl.BlockSpec((1, bt, w), lambda i, j: (i, j, 0))
    tok2 = pl.BlockSpec((1, 2, bt, w), lambda i, j: (i, 0, j, 0))
    s1 = jax.ShapeDtypeStruct((b, t, w), F32)
    s2 = jax.ShapeDtypeStruct((b, 2, t, w), F32)
    return pl.pallas_call(
        _rwkv_prep_kernel,
        out_shape=(s1, s1, s1, s2, s2, s2, s1, s1),
        grid=(b, t // bt),
        in_specs=[pl.BlockSpec((1, bt, cols), lambda i, j: (i, j, 0)),
                  pl.BlockSpec((1, 8, cols), lambda i, j: (i, jnp.maximum(j * hb - 1, 0), 0)),
                  pl.BlockSpec((1, 8, cols), lambda i, j: (i, jnp.minimum((j + 1) * hb, nh - 1), 0)),
                  row(cols), row(cols), row(2 * w), full(LANES, 2 * w), row(2 * w), full(LANES, 2 * w),
                  full(GATE_RANK_PAD, w), row(w), row(w), row(w)],
        out_specs=(tok, tok, tok, tok2, tok2, tok2, tok, tok),
        compiler_params=_params(("parallel", "arbitrary"), 56),
        name="rwkv_prep",
    )(zr, zr, zr, mup, mun, decay_w0.reshape(1, 2 * w), blockdiag(decay_w2), aaa_a0.reshape(1, 2 * w),
      blockdiag(aaa_a2), g2, k_k.reshape(1, w), k_a.reshape(1, w), r_k.reshape(1, w))


def _split3(x):
    hi = x.astype(BF16)
    r1 = x - hi.astype(F32)
    mid = r1.astype(BF16)
    lo = (r1 - mid.astype(F32)).astype(BF16)
    return hi, mid, lo


def _rwkv_scan_kernel(r_ref, v_ref, kk_ref, lw_ref, kd_ref, bb_ref, y_ref, s_ref):
    d = pl.program_id(1)
    c = pl.program_id(2)
    C = CHUNK
    n_pairs = R_WIDTH // LANES

    @pl.when(c == 0)
    def _():
        s_ref[...] = jnp.zeros(s_ref.shape, F32)

    fwd = d == 0
    ti = lax.broadcasted_iota(jnp.int32, (C, C), 0)
    si = lax.broadcasted_iota(jnp.int32, (C, C), 1)
    tri = jnp.where(jnp.where(fwd, ti - si, si - ti) >= 0, 1.0, 0.0).astype(BF16)
    r2 = lax.broadcasted_iota(jnp.int32, (2 * C, 2 * C), 0)
    c2 = lax.broadcasted_iota(jnp.int32, (2 * C, 2 * C), 1)
    same = (r2 // C) == (c2 // C)
    delta = jnp.where(fwd, r2 - c2, c2 - r2)
    strict = same & (delta > 0)
    incl = same & (delta >= 0)
    eye = jnp.where(r2 == c2, 1.0, 0.0)
    lane = lax.broadcasted_iota(jnp.int32, (C, LANES), 1)
    first = lane < R_HEAD

    def stack(x):
        return jnp.concatenate([jnp.where(first, x, 0.0), jnp.where(first, 0.0, x)], axis=0)

    lw_all = lw_ref[0, 0]
    hi, mid, lo = _split3(lw_all)
    cum_all = _dot(tri, hi) + (_dot(tri, mid) + _dot(tri, lo))
    tot_all = jnp.sum(lw_all, axis=0, keepdims=True)

    for p in range(n_pairs):
        sl = slice(p * LANES, (p + 1) * LANES)
        lw, cum, tot = lw_all[:, sl], cum_all[:, sl], tot_all[:, sl]
        r, v, kk = r_ref[0, :, sl], v_ref[0, :, sl], kk_ref[0, :, sl]
        kd, bb = kd_ref[0, 0, :, sl], bb_ref[0, 0, :, sl]
        e_neg = jnp.exp(-cum)
        e_end = jnp.exp(tot - cum)
        at = -kk * jnp.exp(cum - lw)
        rt = r * jnp.exp(cum)
        ar = jnp.concatenate([stack(at), stack(rt)], axis=0).astype(BF16)
        bk = jnp.concatenate([stack(bb * e_neg), stack(kd * e_neg)], axis=0).astype(BF16)
        bk_end = jnp.concatenate([stack(bb * e_end), stack(kd * e_end)], axis=0).astype(BF16)
        pm = _dot(ar, bk, NT)
        lab = jnp.where(strict, pm[:2 * C, :2 * C], 0.0)
        lak = jnp.where(strict, pm[:2 * C, 2 * C:], 0.0)
        mrb = jnp.where(incl, pm[2 * C:, :2 * C], 0.0)
        mrk = jnp.where(incl, pm[2 * C:, 2 * C:], 0.0)
        tm = eye + lab
        lp = lab
        n = 1
        while 2 * n < C:
            lpb = lp.astype(BF16)
            lp = _dot(lpb, lpb)
            tm = tm + _dot(tm.astype(BF16), lp.astype(BF16))
            n *= 2
        s = s_ref[p]
        vs = stack(v).astype(BF16)
        a_s = _dot(ar, s.astype(BF16), NT)
        rhs = a_s[:2 * C] + _dot(lak.astype(BF16), vs)
        u = _dot(tm.astype(BF16), rhs.astype(BF16))
        ub = u.astype(BF16)
        yy = a_s[2 * C:] + _dot(mrb.astype(BF16), ub) + _dot(mrk.astype(BF16), vs)
        y_ref[0, 0, :, sl] = yy[:C] + yy[C:]
        uv = jnp.concatenate([ub, vs], axis=0)
        s_ref[p] = s * jnp.exp(tot) + _dot(uv, bk_end, TN)


def rwkv_scan(r, v, kk, lw, kd, bb):
    b, t, w = r.shape
    nc = t // CHUNK
    tok = pl.BlockSpec((1, CHUNK, w), lambda i, d, c: (i, c + d * (nc - 1 - 2 * c), 0))
    tok2 = pl.BlockSpec((1, 1, CHUNK, w), lambda i, d, c: (i, d, c + d * (nc - 1 - 2 * c), 0))
    return pl.pallas_call(
        _rwkv_scan_kernel,
        out_shape=jax.ShapeDtypeStruct((b, 2, t, w), F32),
        grid=(b, 2, nc),
        in_specs=[tok, tok, tok, tok2, tok2, tok2],
        out_specs=tok2,
        scratch_shapes=[pltpu.VMEM((w // LANES, LANES, LANES), F32)],
        compiler_params=_params(("parallel", "parallel", "arbitrary"), 48),
        name="rwkv_scan",
    )(r, v, kk, lw, kd, bb)


def _rwkv_post_kernel(y_ref, bonus_ref, g_ref, w_ref, b_ref, o_ref):
    y = y_ref[0, 0] + y_ref[0, 1]
    ones = _group_ones(R_HEAD)
    mu = _group_sum(y, ones) * (1.0 / R_HEAD)
    dlt = y - mu
    var = _group_sum(dlt * dlt, ones) * (1.0 / R_HEAD)
    yn = dlt * lax.rsqrt(var + LNX_EPS) * w_ref[...] + b_ref[...]
    o_ref[0] = ((yn + bonus_ref[0]) * g_ref[0]).astype(o_ref.dtype)


def rwkv_post(y, bonus, g, lnx_w, lnx_b, bt=512):
    b, _, t, w = y.shape
    bt = min(bt, t)
    tok = pl.BlockSpec((1, bt, w), lambda i, j: (i, j, 0))
    row = pl.BlockSpec((1, w), lambda i, j: (0, 0))
    return pl.pallas_call(
        _rwkv_post_kernel,
        out_shape=jax.ShapeDtypeStruct((b, t, w), BF16),
        grid=(b, t // bt),
        in_specs=[pl.BlockSpec((1, 2, bt, w), lambda i, j: (i, 0, j, 0)), tok, tok, row, row],
        out_specs=tok,
        compiler_params=_params(("parallel", "arbitrary"), 40),
        name="rwkv_post",
    )(y, bonus, g, lnx_w.reshape(1, w), lnx_b.reshape(1, w))


def _merge_kernel(oa_ref, or_ref, om_ref, pa_ref, pr_ref, pm_ref, ga_ref, gr_ref, gm_ref, o_ref):
    acc = ga_ref[...].astype(F32) * _dot(oa_ref[...], pa_ref[...])
    acc += gr_ref[...].astype(F32) * _dot(or_ref[...], pr_ref[...])
    acc += gm_ref[...].astype(F32) * _dot(om_ref[...], pm_ref[...])
    o_ref[...] = acc.astype(o_ref.dtype)


def gated_merge(o_a, o_r, o_m, p_a, p_r, p_m, gates, bm=1024, bn=512):
    m, k = o_a.shape
    n = p_a.shape[1]
    bm, bn = min(bm, m), min(bn, n)
    nb = n // bn
    a_spec = pl.BlockSpec((bm, k), lambda i, j: (i, 0))
    w_spec = pl.BlockSpec((k, bn), lambda i, j: (0, j))
    gate = lambda o: pl.BlockSpec((bm, bn), lambda i, j: (i, j + o * nb))
    return pl.pallas_call(
        _merge_kernel,
        out_shape=jax.ShapeDtypeStruct((m, n), BF16),
        grid=(m // bm, nb),
        in_specs=[a_spec, a_spec, a_spec, w_spec, w_spec, w_spec, gate(0), gate(1), gate(2)],
        out_specs=pl.BlockSpec((bm, bn), lambda i, j: (i, j)),
        compiler_params=_params(("parallel", "arbitrary"), 48),
        name="gated_merge",
    )(o_a, o_r, o_m, p_a, p_r, p_m, gates, gates, gates)


def _ffn_up_kernel(h_ref, hp_ref, hn_ref, wg_ref, wv_ref, cg_ref, cv_ref, bg_ref, bv_ref, o_ref):
    j = pl.program_id(1)
    nt = pl.num_programs(1)
    h = h_ref[0]
    last = hp_ref.shape[1] - 1

    def branch(w_ref, c_ref, b_ref):
        u = _dot(h, w_ref[...])
        prev_row = jnp.where(j > 0, _dot(hp_ref[0], w_ref[...])[last:last + 1], 0.0)
        next_row = jnp.where(j < nt - 1, _dot(hn_ref[0], w_ref[...])[0:1], 0.0)
        up, dn = _shift_rows(u, prev_row, next_row)
        c = c_ref[...]
        return c[0:1] * up + c[1:2] * u + c[2:3] * dn + b_ref[...]

    gt = branch(wg_ref, cg_ref, bg_ref)
    val = branch(wv_ref, cv_ref, bv_ref)
    o_ref[0] = (gt * jax.nn.sigmoid(gt) * val).astype(o_ref.dtype)


def ffn_up(h, w_up, conv_w, conv_b, bm=1024, bn=512):
    b, t, k = h.shape
    bm = min(bm, t)
    nb = D_FF // bn
    halo = 16
    hb = bm // halo
    nh = t // halo
    return pl.pallas_call(
        _ffn_up_kernel,
        out_shape=jax.ShapeDtypeStruct((b, t, D_FF), BF16),
        grid=(b, t // bm, nb),
        in_specs=[pl.BlockSpec((1, bm, k), lambda i, j, n: (i, j, 0)),
                  pl.BlockSpec((1, halo, k), lambda i, j, n: (i, jnp.maximum(j * hb - 1, 0), 0)),
                  pl.BlockSpec((1, halo, k), lambda i, j, n: (i, jnp.minimum((j + 1) * hb, nh - 1), 0)),
                  pl.BlockSpec((k, bn), lambda i, j, n: (0, n)),
                  pl.BlockSpec((k, bn), lambda i, j, n: (0, n + nb)),
                  pl.BlockSpec((3, bn), lambda i, j, n: (0, n)),
                  pl.BlockSpec((3, bn), lambda i, j, n: (0, n + nb)),
                  pl.BlockSpec((1, bn), lambda i, j, n: (0, n)),
                  pl.BlockSpec((1, bn), lambda i, j, n: (0, n + nb))],
        out_specs=pl.BlockSpec((1, bm, bn), lambda i, j, n: (i, j, n)),
        compiler_params=_params(("parallel", "parallel", "arbitrary"), 56),
        name="ffn_up",
    )(h, h, h, w_up, w_up, conv_w, conv_w, conv_b.reshape(1, -1), conv_b.reshape(1, -1))


def _rope_tables(t):
    half = A_HEAD_DIM // 2
    inv = jnp.power(ROPE_THETA, -jnp.arange(half, dtype=F32) / half)
    ang = jnp.arange(t, dtype=F32)[:, None] * inv[None, :]
    reps = LANES // half
    return jnp.tile(jnp.cos(ang), (1, reps)), jnp.tile(jnp.sin(ang), (1, reps))


def _layer(x, mem, p, w, layer_idx):
    b, t, d = x.shape
    m = b * t
    lam_init = 0.8 - 0.6 * math.exp(-0.3 * layer_idx)
    x2 = x.reshape(m, d)
    h = rmsnorm_rows(x2, p["g_mix"])
    zqkv = matmul(h, w["w_qkv"], BF16).reshape(b, t, -1)
    zr = matmul(h, w["w_r"], F32).reshape(b, t, -1)
    zm = matmul(h, w["w_m"], BF16).reshape(b, t, -1)
    gates = matmul(h, w["w_g"], BF16, act="sigmoid")
    cos, sin = _rope_tables(t)
    qk = qk_prep(zqkv, cos, sin, p["q_norm"], p["k_norm"])
    o_a = diff_attention(qk, zqkv, p["lambda_q1"], p["lambda_k1"], p["lambda_q2"], p["lambda_k2"],
                         p["attn_subln"], lam_init)
    r, v, kk, lw, kd, bb, g, bonus = rwkv_prep(zr, p["mu_prev"], p["mu_next"], p["decay_w0"], p["decay_w2"],
                                               p["aaa_a0"], p["aaa_a2"], p["gate_g2"], p["k_k"], p["k_a"],
                                               p["r_k"])
    y = rwkv_scan(r, v, kk, lw, kd, bb)
    o_r = rwkv_post(y, bonus, g, p["lnx_w"], p["lnx_b"])
    hm = rmsnorm_rows(mem.reshape(b * N_MEM, d), p["g_memnorm"])
    kv = matmul(hm, w["w_mem_kv"], F32).reshape(b, N_MEM, -1)
    o_m = memory_attention(zm, kv, p["mq_norm"], p["mk_norm"])
    merged = gated_merge(o_a.reshape(m, -1), o_r.reshape(m, -1), o_m.reshape(m, -1),
                         w["p_attn"], w["p_rwkv"], w["p_mem"], gates)
    x2 = matmul_residual(merged, w["w_o"], x2)
    h2 = rmsnorm_rows(x2, p["g_ffn"]).reshape(b, t, d)
    act = ffn_up(h2, w["w_up"], p["conv_w"], p["conv_b"])
    x2 = matmul_residual(act.reshape(m, -1), w["w_down"], x2)
    return x2.reshape(b, t, d)


def _prepare_weights(p):
    w_in = p["w_in"]
    c0 = 3 * A_WIDTH
    c1 = c0 + R_COLS
    c2 = c1 + M_WIDTH
    return {
        "w_qkv": w_in[:, :c0].astype(BF16),
        "w_r": jnp.pad(w_in[:, c0:c1], ((0, 0), (0, R_COLS_PAD - R_COLS))).astype(BF16),
        "w_m": w_in[:, c1:c2].astype(BF16),
        "w_g": w_in[:, c2:].astype(BF16),
        "w_mem_kv": p["w_mem_kv"].astype(BF16),
        "p_attn": p["p_attn"].astype(BF16),
        "p_rwkv": p["p_rwkv"].astype(BF16),
        "p_mem": p["p_mem"].astype(BF16),
        "w_o": p["w_o"].astype(BF16),
        "w_up": p["w_up"].astype(BF16),
        "w_down": p["w_down"].astype(BF16),
    }


def kernel(x_prompt, x_sample, mem_prompt, mem_sample, g_mix, w_in, q_norm, k_norm, lambda_q1, lambda_k1, lambda_q2, lambda_k2, attn_subln, mu_prev, mu_next, decay_w0, decay_w2, aaa_a0, aaa_a2, gate_g2, k_k, k_a, r_k, lnx_w, lnx_b, g_memnorm, w_mem_kv, mq_norm, mk_norm, p_attn, p_rwkv, p_mem, w_o, g_ffn, w_up, conv_w, conv_b, w_down):
    params = dict(g_mix=g_mix, w_in=w_in, q_norm=q_norm, k_norm=k_norm, lambda_q1=lambda_q1,
                  lambda_k1=lambda_k1, lambda_q2=lambda_q2, lambda_k2=lambda_k2, attn_subln=attn_subln,
                  mu_prev=mu_prev, mu_next=mu_next, decay_w0=decay_w0, decay_w2=decay_w2, aaa_a0=aaa_a0,
                  aaa_a2=aaa_a2, gate_g2=gate_g2, k_k=k_k, k_a=k_a, r_k=r_k, lnx_w=lnx_w, lnx_b=lnx_b,
                  g_memnorm=g_memnorm, w_mem_kv=w_mem_kv, mq_norm=mq_norm, mk_norm=mk_norm, p_attn=p_attn,
                  p_rwkv=p_rwkv, p_mem=p_mem, w_o=w_o, g_ffn=g_ffn, w_up=w_up, conv_w=conv_w, conv_b=conv_b,
                  w_down=w_down)
    layers = [{name: arr[l] for name, arr in params.items()} for l in range(g_mix.shape[0])]
    weights = [_prepare_weights(p) for p in layers]
    outs = []
    for x, mem in ((x_prompt, mem_prompt), (x_sample, mem_sample)):
        for l, (p, w) in enumerate(zip(layers, weights)):
            x = _layer(x, mem, p, w, l)
        outs.append(x)
    return tuple(outs)
```

```python
import functools
import math

import jax
import jax.numpy as jnp
from jax import lax
from jax.experimental import pallas as pl
from jax.experimental.pallas import tpu as pltpu

F32 = jnp.float32
BF16 = jnp.bfloat16

D_MODEL = 2048
A_HEADS = 8
A_HEAD_DIM = 64
A_WIDTH = 1024
ROPE_THETA = 10000.0
R_HEAD = 64
R_WIDTH = 1024
R_COLS = 3488
R_COLS_PAD = 3584
GATE_RANK = 160
GATE_RANK_PAD = 256
LNX_EPS = 64e-5
N_MEM = 256
M_HEADS = 4
M_HEAD_DIM = 256
M_WIDTH = 1024
D_FF = 5632
EPS = 1e-6
LANES = 128
CHUNK = 64
LOG2E = 1.4426950408889634

NN = (((1,), (0,)), ((), ()))
NT = (((1,), (1,)), ((), ()))
TN = (((0,), (0,)), ((), ()))


def _params(sem, vmem_mb):
    return pltpu.CompilerParams(dimension_semantics=sem, vmem_limit_bytes=vmem_mb << 20)


def _dot(a, b, dims=NN):
    return lax.dot_general(a, b, dims, preferred_element_type=F32)


def _rmsnorm_kernel(x_ref, g_ref, o_ref):
    x = x_ref[...]
    ms = jnp.mean(x * x, axis=-1, keepdims=True)
    o_ref[...] = (x * lax.rsqrt(ms + EPS) * g_ref[...]).astype(o_ref.dtype)


def rmsnorm_rows(x2, g, bm=512):
    m, d = x2.shape
    bm = min(bm, m)
    return pl.pallas_call(
        _rmsnorm_kernel,
        out_shape=jax.ShapeDtypeStruct((m, d), BF16),
        grid=(m // bm,),
        in_specs=[pl.BlockSpec((bm, d), lambda i: (i, 0)),
                  pl.BlockSpec((1, d), lambda i: (0, 0))],
        out_specs=pl.BlockSpec((bm, d), lambda i: (i, 0)),
        compiler_params=_params(("parallel",), 40),
        name="rmsnorm_rows",
    )(x2, g.reshape(1, d))


def _mm_kernel(a_ref, b_ref, o_ref, *, act):
    acc = _dot(a_ref[...], b_ref[...])
    if act == "sigmoid":
        acc = jax.nn.sigmoid(acc)
    o_ref[...] = acc.astype(o_ref.dtype)


def matmul(a, b, out_dtype, act=None, bm=1024, bn=512):
    m, k = a.shape
    n = b.shape[1]
    bm, bn = min(bm, m), min(bn, n)
    return pl.pallas_call(
        functools.partial(_mm_kernel, act=act),
        out_shape=jax.ShapeDtypeStruct((m, n), out_dtype),
        grid=(m // bm, n // bn),
        in_specs=[pl.BlockSpec((bm, k), lambda i, j: (i, 0)),
                  pl.BlockSpec((k, bn), lambda i, j: (0, j))],
        out_specs=pl.BlockSpec((bm, bn), lambda i, j: (i, j)),
        compiler_params=_params(("parallel", "arbitrary"), 48),
        name="matmul",
    )(a, b)


def _mm_res_kernel(a_ref, b_ref, r_ref, o_ref):
    o_ref[...] = r_ref[...] + _dot(a_ref[...], b_ref[...])


def matmul_residual(a, b, res, bm=1024, bn=512):
    m, k = a.shape
    n = b.shape[1]
    bm, bn = min(bm, m), min(bn, n)
    return pl.pallas_call(
        _mm_res_kernel,
        out_shape=jax.ShapeDtypeStruct((m, n), F32),
        grid=(m // bm, n // bn),
        in_specs=[pl.BlockSpec((bm, k), lambda i, j: (i, 0)),
                  pl.BlockSpec((k, bn), lambda i, j: (0, j)),
                  pl.BlockSpec((bm, bn), lambda i, j: (i, j))],
        out_specs=pl.BlockSpec((bm, bn), lambda i, j: (i, j)),
        compiler_params=_params(("parallel", "arbitrary"), 56),
        name="matmul_residual",
    )(a, b, res)


def _group_ones(group):
    r = lax.broadcasted_iota(jnp.int32, (2 * LANES, LANES), 0)
    c = lax.broadcasted_iota(jnp.int32, (2 * LANES, LANES), 1)
    return jnp.where((r % LANES) // group == c // group, 1.0, 0.0).astype(BF16)


def _group_sum(x, ones):
    hi = x.astype(BF16)
    lo = (x - hi.astype(F32)).astype(BF16)
    outs = []
    for c in range(x.shape[1] // LANES):
        sl = slice(LANES * c, LANES * (c + 1))
        outs.append(_dot(jnp.concatenate([hi[:, sl], lo[:, sl]], axis=1), ones))
    return outs[0] if len(outs) == 1 else jnp.concatenate(outs, axis=1)


def _tile_lanes(x, reps):
    return x if reps == 1 else jnp.concatenate([x] * reps, axis=1)


def _shift_rows(u, prev_row, next_row):
    n = u.shape[0]
    row = lax.broadcasted_iota(jnp.int32, u.shape, 0)
    up = jnp.where(row == 0, prev_row, pltpu.roll(u, 1, 0))
    dn = jnp.where(row == n - 1, next_row, pltpu.roll(u, n - 1, 0))
    return up, dn


def _qk_prep_kernel(x_ref, cos_ref, sin_ref, gq_ref, gk_ref, o_ref):
    c = pl.program_id(2)
    x = x_ref[0].astype(F32)
    reps = x.shape[1] // LANES
    ss = _group_sum(x * x, _group_ones(A_HEAD_DIM))
    g = jnp.where(c == 0, gq_ref[...], gk_ref[...])
    xn = x * lax.rsqrt(ss * (1.0 / A_HEAD_DIM) + EPS) * _tile_lanes(g, reps)
    lane = lax.broadcasted_iota(jnp.int32, x.shape, 1)
    half = A_HEAD_DIM // 2
    rot = jnp.where(lane % A_HEAD_DIM < half,
                    -pltpu.roll(xn, x.shape[1] - half, 1), pltpu.roll(xn, half, 1))
    y = xn * _tile_lanes(cos_ref[...], reps) + rot * _tile_lanes(sin_ref[...], reps)
    scale = jnp.where(c == 0, A_HEAD_DIM ** -0.5 * LOG2E, 1.0)
    o_ref[0] = (y * scale).astype(o_ref.dtype)


def qk_prep(zqkv, cos, sin, q_norm, k_norm, bt=512):
    b, t, _ = zqkv.shape
    bt = min(bt, t)
    gq = jnp.tile(q_norm.reshape(1, A_HEAD_DIM), (1, LANES // A_HEAD_DIM))
    gk = jnp.tile(k_norm.reshape(1, A_HEAD_DIM), (1, LANES // A_HEAD_DIM))
    return pl.pallas_call(
        _qk_prep_kernel,
        out_shape=jax.ShapeDtypeStruct((b, t, 2 * A_WIDTH), BF16),
        grid=(b, t // bt, 2),
        in_specs=[pl.BlockSpec((1, bt, A_WIDTH), lambda i, j, c: (i, j, c)),
                  pl.BlockSpec((bt, LANES), lambda i, j, c: (j, 0)),
                  pl.BlockSpec((bt, LANES), lambda i, j, c: (j, 0)),
                  pl.BlockSpec((1, LANES), lambda i, j, c: (0, 0)),
                  pl.BlockSpec((1, LANES), lambda i, j, c: (0, 0))],
        out_specs=pl.BlockSpec((1, bt, A_WIDTH), lambda i, j, c: (i, j, c)),
        compiler_params=_params(("parallel", "parallel", "arbitrary"), 40),
        name="qk_prep",
    )(zqkv, cos, sin, gq, gk)


def _dattn_kernel(lq1_ref, lk1_ref, lq2_ref, lk2_ref, g_ref, q_ref, k_ref, v_ref, o_ref,
                  m_ref, acc_ref, s_ref, *, bk, lam_init):
    bq = q_ref.shape[1]
    nk = k_ref.shape[1] // bk
    q = q_ref[0]
    lane = lax.broadcasted_iota(jnp.int32, q.shape, 1)
    zero = jnp.zeros_like(q)
    first = lane < A_HEAD_DIM
    qq = jnp.concatenate([jnp.where(first, q, zero), jnp.where(first, zero, q)], axis=0)
    m_ref[...] = jnp.full(m_ref.shape, -jnp.inf, F32)
    acc_ref[...] = jnp.zeros(acc_ref.shape, F32)
    ones = jnp.ones((bk, LANES), BF16)

    def scores(j):
        start = pl.multiple_of(j * bk, bk)
        return _dot(qq, k_ref[0, pl.ds(start, bk), :], NT)

    def accumulate(j, s):
        start = pl.multiple_of(j * bk, bk)
        vs = v_ref[0, pl.ds(start, bk), :]
        m_prev = m_ref[...]
        m_new = jnp.maximum(m_prev, jnp.max(s, axis=-1, keepdims=True))
        alpha = jnp.exp2(m_prev - m_new)
        p = jnp.exp2(s - _tile_lanes(m_new, bk // LANES))
        pv = _dot(p.astype(BF16), jnp.concatenate([vs, ones], axis=1))
        acc_ref[...] = _tile_lanes(alpha, 2) * acc_ref[...] + pv
        m_ref[...] = m_new

    s_ref[0] = scores(0)

    def body(i, carry):
        j = 2 * i
        s_ref[1] = scores(j + 1)
        accumulate(j, s_ref[0])
        s_ref[0] = scores(j + 2)
        accumulate(j + 1, s_ref[1])
        return carry

    lax.fori_loop(0, nk // 2 - 1, body, 0)
    s_ref[1] = scores(nk - 1)
    accumulate(nk - 2, s_ref[0])
    accumulate(nk - 1, s_ref[1])
    lam = (jnp.exp(jnp.sum(lq1_ref[...] * lk1_ref[...], keepdims=True))
           - jnp.exp(jnp.sum(lq2_ref[...] * lk2_ref[...], keepdims=True)) + lam_init)
    o12 = acc_ref[:, :LANES] / acc_ref[:, LANES:]
    o = o12[:bq] - lam * o12[bq:]
    ms = jnp.mean(o * o, axis=-1, keepdims=True)
    y = (o * lax.rsqrt(ms + EPS) * g_ref[...]) * (1.0 - lam_init)
    o_ref[0] = y.astype(o_ref.dtype)


def diff_attention(qk, zqkv, lq1, lk1, lq2, lk2, sub_g, lam_init, bq=512, bk=512):
    b, t, _ = qk.shape
    bq, bk = min(bq, t), min(bk, t // 2)
    assert (t // bk) % 2 == 0, "the key loop is pipelined over pairs of blocks"
    vec = lambda a: a.reshape(1, -1)
    small = lambda n: pl.BlockSpec((1, n), lambda i, h, j: (0, 0))
    return pl.pallas_call(
        functools.partial(_dattn_kernel, bk=bk, lam_init=lam_init),
        out_shape=jax.ShapeDtypeStruct((b, t, A_WIDTH), BF16),
        grid=(b, A_HEADS, t // bq),
        in_specs=[small(A_HEAD_DIM), small(A_HEAD_DIM), small(A_HEAD_DIM), small(A_HEAD_DIM),
                  small(LANES),
                  pl.BlockSpec((1, bq, LANES), lambda i, h, j: (i, j, h)),
                  pl.BlockSpec((1, t, LANES), lambda i, h, j: (i, 0, A_HEADS + h)),
                  pl.BlockSpec((1, t, LANES), lambda i, h, j: (i, 0, 2 * A_HEADS + h))],
        out_specs=pl.BlockSpec((1, bq, LANES), lambda i, h, j: (i, j, h)),
        scratch_shapes=[pltpu.VMEM((2 * bq, LANES), F32), pltpu.VMEM((2 * bq, 2 * LANES), F32),
                        pltpu.VMEM((2, 2 * bq, bk), F32)],
        compiler_params=_params(("parallel", "parallel", "arbitrary"), 48),
        name="diff_attention",
    )(vec(lq1), vec(lk1), vec(lq2), vec(lk2), vec(sub_g), qk, qk, zqkv)


def _mem_attn_kernel(q_ref, kv_ref, gq_ref, gk_ref, o_ref):
    outs = []
    for h in range(M_HEADS):
        sl = slice(h * M_HEAD_DIM, (h + 1) * M_HEAD_DIM)
        qh = q_ref[0, :, sl].astype(F32)
        qn = qh * lax.rsqrt(jnp.mean(qh * qh, axis=-1, keepdims=True) + EPS) * gq_ref[...]
        kh = kv_ref[0, :, sl]
        kn = kh * lax.rsqrt(jnp.mean(kh * kh, axis=-1, keepdims=True) + EPS) * gk_ref[...]
        vh = kv_ref[0, :, M_WIDTH + h * M_HEAD_DIM:M_WIDTH + (h + 1) * M_HEAD_DIM]
        s = _dot((qn * M_HEAD_DIM ** -0.5).astype(BF16), kn.astype(BF16), NT)
        p = jnp.exp(s - jnp.max(s, axis=-1, keepdims=True))
        o = _dot(p.astype(BF16), vh.astype(BF16)) / jnp.sum(p, axis=-1, keepdims=True)
        outs.append(o)
    o_ref[0] = jnp.concatenate(outs, axis=1).astype(o_ref.dtype)


def memory_attention(zm, kv, mq_norm, mk_norm, bt=512):
    b, t, _ = zm.shape
    bt = min(bt, t)
    return pl.pallas_call(
        _mem_attn_kernel,
        out_shape=jax.ShapeDtypeStruct((b, t, M_WIDTH), BF16),
        grid=(b, t // bt),
        in_specs=[pl.BlockSpec((1, bt, M_WIDTH), lambda i, j: (i, j, 0)),
                  pl.BlockSpec((1, N_MEM, 2 * M_WIDTH), lambda i, j: (i, 0, 0)),
                  pl.BlockSpec((1, M_HEAD_DIM), lambda i, j: (0, 0)),
                  pl.BlockSpec((1, M_HEAD_DIM), lambda i, j: (0, 0))],
        out_specs=pl.BlockSpec((1, bt, M_WIDTH), lambda i, j: (i, j, 0)),
        compiler_params=_params(("parallel", "arbitrary"), 40),
        name="memory_attention",
    )(zm, kv, mq_norm.reshape(1, -1), mk_norm.reshape(1, -1))


def _rwkv_prep_kernel(z_ref, zp_ref, zn_ref, mup_ref, mun_ref, w0_ref, w2_ref, a0_ref, a2_ref, g2_ref,
                      kk_w_ref, ka_ref, rk_ref,
                      r_ref, v_ref, kk_ref, lw_ref, kd_ref, bb_ref, g_ref, bonus_ref):
    j = pl.program_id(1)
    nt = pl.num_programs(1)
    z = z_ref[0]
    prev_row = jnp.where(j > 0, zp_ref[0, 7:8, :], 0.0)
    next_row = jnp.where(j < nt - 1, zn_ref[0, 0:1, :], 0.0)
    zp, zn = _shift_rows(z, prev_row, next_row)
    z = z + mup_ref[...] * (zp - z) + mun_ref[...] * (zn - z)
    w = R_WIDTH
    r, k, v = z[:, 0:w], z[:, w:2 * w], z[:, 2 * w:3 * w]
    zw = z[:, 3 * w:3 * w + LANES]
    za = z[:, 3 * w + LANES:3 * w + 2 * LANES]
    zg = z[:, 3 * w + 2 * LANES:]
    wlin = w0_ref[...] + _dot(jnp.tanh(zw).astype(BF16), w2_ref[...])
    neg = -wlin
    softplus = jnp.maximum(neg, 0.0) + jnp.log(1.0 + jnp.exp(-jnp.abs(neg)))
    lw = -jnp.exp(-softplus - 0.5)
    a = jax.nn.sigmoid(a0_ref[...] + _dot(za.astype(BF16), a2_ref[...]))
    g = _dot(jax.nn.sigmoid(zg).astype(BF16), g2_ref[...])
    ones = _group_ones(R_HEAD)
    kk = k * kk_w_ref[...]
    norm = jnp.sqrt(_group_sum(kk * kk, ones))
    kk = kk / jnp.maximum(norm, 1e-12)
    ka = ka_ref[...]
    kd0 = k * (1.0 + (a[:, :w] - 1.0) * ka)
    kd1 = k * (1.0 + (a[:, w:] - 1.0) * ka)
    bonus = _group_sum(r * (kd0 + kd1) * rk_ref[...], ones) * v
    r_ref[0] = r
    v_ref[0] = v
    kk_ref[0] = kk
    lw_ref[0, 0] = lw[:, :w]
    lw_ref[0, 1] = lw[:, w:]
    kd_ref[0, 0] = kd0
    kd_ref[0, 1] = kd1
    bb_ref[0, 0] = kk * a[:, :w]
    bb_ref[0, 1] = kk * a[:, w:]
    g_ref[0] = g
    bonus_ref[0] = bonus


def rwkv_prep(zr, mu_prev, mu_next, decay_w0, decay_w2, aaa_a0, aaa_a2, gate_g2, k_k, k_a, r_k, bt=256):
    b, t, cols = zr.shape
    bt = min(bt, t)
    w = R_WIDTH
    pad = cols - R_COLS
    mup = jnp.pad(mu_prev, (0, pad)).reshape(1, cols)
    mun = jnp.pad(mu_next, (0, pad)).reshape(1, cols)

    def blockdiag(m):
        z = jnp.zeros_like(m[0])
        return jnp.concatenate([jnp.concatenate([m[0], z], axis=1),
                                jnp.concatenate([z, m[1]], axis=1)], axis=0).astype(BF16)

    g2 = jnp.pad(gate_g2, ((0, GATE_RANK_PAD - GATE_RANK), (0, 0))).astype(BF16)
    row = lambda n: pl.BlockSpec((1, n), lambda i, j: (0, 0))
    full = lambda r_, c_: pl.BlockSpec((r_, c_), lambda i, j: (0, 0))
    hb = bt // 8
    nh = t // 8
    tok = pl.BlockSpec((1, bt, w), lambda i, j: (i, j, 0))
    tok2 = pl.BlockSpec((1, 2, bt, w), lambda i, j: (i, 0, j, 0))
    s1 = jax.ShapeDtypeStruct((b, t, w), F32)
    s2 = jax.ShapeDtypeStruct((b, 2, t, w), F32)
    return pl.pallas_call(
        _rwkv_prep_kernel,
        out_shape=(s1, s1, s1, s2, s2, s2, s1, s1),
        grid=(b, t // bt),
        in_specs=[pl.BlockSpec((1, bt, cols), lambda i, j: (i, j, 0)),
                  pl.BlockSpec((1, 8, cols), lambda i, j: (i, jnp.maximum(j * hb - 1, 0), 0)),
                  pl.BlockSpec((1, 8, cols), lambda i, j: (i, jnp.minimum((j + 1) * hb, nh - 1), 0)),
                  row(cols), row(cols), row(2 * w), full(LANES, 2 * w), row(2 * w), full(LANES, 2 * w),
                  full(GATE_RANK_PAD, w), row(w), row(w), row(w)],
        out_specs=(tok, tok, tok, tok2, tok2, tok2, tok, tok),
        compiler_params=_params(("parallel", "arbitrary"), 56),
        name="rwkv_prep",
    )(zr, zr, zr, mup, mun, decay_w0.reshape(1, 2 * w), blockdiag(decay_w2), aaa_a0.reshape(1, 2 * w),
      blockdiag(aaa_a2), g2, k_k.reshape(1, w), k_a.reshape(1, w), r_k.reshape(1, w))


def _split3(x):
    hi = x.astype(BF16)
    r1 = x - hi.astype(F32)
    mid = r1.astype(BF16)
    lo = (r1 - mid.astype(F32)).astype(BF16)
    return hi, mid, lo


def _rwkv_scan_kernel(r_ref, v_ref, kk_ref, lw_ref, kd_ref, bb_ref, y_ref, s_ref):
    d = pl.program_id(1)
    c = pl.program_id(2)
    C = CHUNK
    n_pairs = R_WIDTH // LANES

    @pl.when(c == 0)
    def _():
        s_ref[...] = jnp.zeros(s_ref.shape, F32)

    fwd = d == 0
    ti = lax.broadcasted_iota(jnp.int32, (C, C), 0)
    si = lax.broadcasted_iota(jnp.int32, (C, C), 1)
    tri = jnp.where(jnp.where(fwd, ti - si, si - ti) >= 0, 1.0, 0.0).astype(BF16)
    r2 = lax.broadcasted_iota(jnp.int32, (2 * C, 2 * C), 0)
    c2 = lax.broadcasted_iota(jnp.int32, (2 * C, 2 * C), 1)
    same = (r2 // C) == (c2 // C)
    delta = jnp.where(fwd, r2 - c2, c2 - r2)
    strict = same & (delta > 0)
    incl = same & (delta >= 0)
    eye = jnp.where(r2 == c2, 1.0, 0.0)
    lane = lax.broadcasted_iota(jnp.int32, (C, LANES), 1)
    first = lane < R_HEAD

    def stack(x):
        return jnp.concatenate([jnp.where(first, x, 0.0), jnp.where(first, 0.0, x)], axis=0)

    lw_all = lw_ref[0, 0]
    hi, mid, lo = _split3(lw_all)
    cum_all = _dot(tri, hi) + (_dot(tri, mid) + _dot(tri, lo))
    tot_all = jnp.sum(lw_all, axis=0, keepdims=True)

    pairs = range(n_pairs)
    sls = [slice(p * LANES, (p + 1) * LANES) for p in pairs]
    ar, bk, bk_end, vs = [], [], [], []
    for sl in sls:
        lw, cum, tot = lw_all[:, sl], cum_all[:, sl], tot_all[:, sl]
        r, v, kk = r_ref[0, :, sl], v_ref[0, :, sl], kk_ref[0, :, sl]
        kd, bb = kd_ref[0, 0, :, sl], bb_ref[0, 0, :, sl]
        e_neg = jnp.exp(-cum)
        e_end = jnp.exp(tot - cum)
        at = -kk * jnp.exp(cum - lw)
        rt = r * jnp.exp(cum)
        ar.append(jnp.concatenate([stack(at), stack(rt)], axis=0).astype(BF16))
        bk.append(jnp.concatenate([stack(bb * e_neg), stack(kd * e_neg)], axis=0).astype(BF16))
        bk_end.append(jnp.concatenate([stack(bb * e_end), stack(kd * e_end)], axis=0).astype(BF16))
        vs.append(stack(v).astype(BF16))
    pm = [_dot(ar[p], bk[p], NT) for p in pairs]
    lab = [jnp.where(strict, pm[p][:2 * C, :2 * C], 0.0) for p in pairs]
    lak_mrk = [jnp.concatenate([jnp.where(strict, pm[p][:2 * C, 2 * C:], 0.0),
                                jnp.where(incl, pm[p][2 * C:, 2 * C:], 0.0)], axis=0).astype(BF16)
               for p in pairs]
    mrb = [jnp.where(incl, pm[p][2 * C:, :2 * C], 0.0).astype(BF16) for p in pairs]
    tm = [eye + lab[p] for p in pairs]
    lp = lab
    n = 1
    while 2 * n < C:
        lpb = [lp[p].astype(BF16) for p in pairs]
        lp = [_dot(lpb[p], lpb[p]) for p in pairs]
        tm = [tm[p] + _dot(tm[p].astype(BF16), lp[p].astype(BF16)) for p in pairs]
        n *= 2
    s = [s_ref[p] for p in pairs]
    a_s = [_dot(ar[p], s[p].astype(BF16), NT) for p in pairs]
    lv = [_dot(lak_mrk[p], vs[p]) for p in pairs]
    ub = [_dot(tm[p].astype(BF16), (a_s[p][:2 * C] + lv[p][:2 * C]).astype(BF16)).astype(BF16)
          for p in pairs]
    yy = [a_s[p][2 * C:] + lv[p][2 * C:] + _dot(mrb[p], ub[p]) for p in pairs]
    for p in pairs:
        y_ref[0, 0, :, sls[p]] = yy[p][:C] + yy[p][C:]
    for p in pairs:
        uv = jnp.concatenate([ub[p], vs[p]], axis=0)
        s_ref[p] = s[p] * jnp.exp(tot_all[:, sls[p]]) + _dot(uv, bk_end[p], TN)


def rwkv_scan(r, v, kk, lw, kd, bb):
    b, t, w = r.shape
    nc = t // CHUNK
    tok = pl.BlockSpec((1, CHUNK, w), lambda i, d, c: (i, c + d * (nc - 1 - 2 * c), 0))
    tok2 = pl.BlockSpec((1, 1, CHUNK, w), lambda i, d, c: (i, d, c + d * (nc - 1 - 2 * c), 0))
    return pl.pallas_call(
        _rwkv_scan_kernel,
        out_shape=jax.ShapeDtypeStruct((b, 2, t, w), F32),
        grid=(b, 2, nc),
        in_specs=[tok, tok, tok, tok2, tok2, tok2],
        out_specs=tok2,
        scratch_shapes=[pltpu.VMEM((w // LANES, LANES, LANES), F32)],
        compiler_params=_params(("parallel", "parallel", "arbitrary"), 48),
        name="rwkv_scan",
    )(r, v, kk, lw, kd, bb)


def _rwkv_post_kernel(y_ref, bonus_ref, g_ref, w_ref, b_ref, o_ref):
    y = y_ref[0, 0] + y_ref[0, 1]
    ones = _group_ones(R_HEAD)
    mu = _group_sum(y, ones) * (1.0 / R_HEAD)
    dlt = y - mu
    var = _group_sum(dlt * dlt, ones) * (1.0 / R_HEAD)
    yn = dlt * lax.rsqrt(var + LNX_EPS) * w_ref[...] + b_ref[...]
    o_ref[0] = ((yn + bonus_ref[0]) * g_ref[0]).astype(o_ref.dtype)


def rwkv_post(y, bonus, g, lnx_w, lnx_b, bt=512):
    b, _, t, w = y.shape
    bt = min(bt, t)
    tok = pl.BlockSpec((1, bt, w), lambda i, j: (i, j, 0))
    row = pl.BlockSpec((1, w), lambda i, j: (0, 0))
    return pl.pallas_call(
        _rwkv_post_kernel,
        out_shape=jax.ShapeDtypeStruct((b, t, w), BF16),
        grid=(b, t // bt),
        in_specs=[pl.BlockSpec((1, 2, bt, w), lambda i, j: (i, 0, j, 0)), tok, tok, row, row],
        out_specs=tok,
        compiler_params=_params(("parallel", "arbitrary"), 40),
        name="rwkv_post",
    )(y, bonus, g, lnx_w.reshape(1, w), lnx_b.reshape(1, w))


def _merge_kernel(oa_ref, or_ref, om_ref, pa_ref, pr_ref, pm_ref, ga_ref, gr_ref, gm_ref, o_ref):
    acc = ga_ref[...].astype(F32) * _dot(oa_ref[...], pa_ref[...])
    acc += gr_ref[...].astype(F32) * _dot(or_ref[...], pr_ref[...])
    acc += gm_ref[...].astype(F32) * _dot(om_ref[...], pm_ref[...])
    o_ref[...] = acc.astype(o_ref.dtype)


def gated_merge(o_a, o_r, o_m, p_a, p_r, p_m, gates, bm=1024, bn=512):
    m, k = o_a.shape
    n = p_a.shape[1]
    bm, bn = min(bm, m), min(bn, n)
    nb = n // bn
    a_spec = pl.BlockSpec((bm, k), lambda i, j: (i, 0))
    w_spec = pl.BlockSpec((k, bn), lambda i, j: (0, j))
    gate = lambda o: pl.BlockSpec((bm, bn), lambda i, j: (i, j + o * nb))
    return pl.pallas_call(
        _merge_kernel,
        out_shape=jax.ShapeDtypeStruct((m, n), BF16),
        grid=(m // bm, nb),
        in_specs=[a_spec, a_spec, a_spec, w_spec, w_spec, w_spec, gate(0), gate(1), gate(2)],
        out_specs=pl.BlockSpec((bm, bn), lambda i, j: (i, j)),
        compiler_params=_params(("parallel", "arbitrary"), 48),
        name="gated_merge",
    )(o_a, o_r, o_m, p_a, p_r, p_m, gates, gates, gates)


def _ffn_up_kernel(h_ref, hp_ref, hn_ref, wg_ref, wv_ref, cg_ref, cv_ref, bg_ref, bv_ref, o_ref):
    j = pl.program_id(1)
    nt = pl.num_programs(1)
    h = h_ref[0]
    last = hp_ref.shape[1] - 1

    def branch(w_ref, c_ref, b_ref):
        u = _dot(h, w_ref[...])
        prev_row = jnp.where(j > 0, _dot(hp_ref[0], w_ref[...])[last:last + 1], 0.0)
        next_row = jnp.where(j < nt - 1, _dot(hn_ref[0], w_ref[...])[0:1], 0.0)
        up, dn = _shift_rows(u, prev_row, next_row)
        c = c_ref[...]
        return c[0:1] * up + c[1:2] * u + c[2:3] * dn + b_ref[...]

    gt = branch(wg_ref, cg_ref, bg_ref)
    val = branch(wv_ref, cv_ref, bv_ref)
    o_ref[0] = (gt * jax.nn.sigmoid(gt) * val).astype(o_ref.dtype)


def ffn_up(h, w_up, conv_w, conv_b, bm=1024, bn=512):
    b, t, k = h.shape
    bm = min(bm, t)
    nb = D_FF // bn
    halo = 16
    hb = bm // halo
    nh = t // halo
    return pl.pallas_call(
        _ffn_up_kernel,
        out_shape=jax.ShapeDtypeStruct((b, t, D_FF), BF16),
        grid=(b, t // bm, nb),
        in_specs=[pl.BlockSpec((1, bm, k), lambda i, j, n: (i, j, 0)),
                  pl.BlockSpec((1, halo, k), lambda i, j, n: (i, jnp.maximum(j * hb - 1, 0), 0)),
                  pl.BlockSpec((1, halo, k), lambda i, j, n: (i, jnp.minimum((j + 1) * hb, nh - 1), 0)),
                  pl.BlockSpec((k, bn), lambda i, j, n: (0, n)),
                  pl.BlockSpec((k, bn), lambda i, j, n: (0, n + nb)),
                  pl.BlockSpec((3, bn), lambda i, j, n: (0, n)),
                  pl.BlockSpec((3, bn), lambda i, j, n: (0, n + nb)),
                  pl.BlockSpec((1, bn), lambda i, j, n: (0, n)),
                  pl.BlockSpec((1, bn), lambda i, j, n: (0, n + nb))],
        out_specs=pl.BlockSpec((1, bm, bn), lambda i, j, n: (i, j, n)),
        compiler_params=_params(("parallel", "parallel", "arbitrary"), 56),
        name="ffn_up",
    )(h, h, h, w_up, w_up, conv_w, conv_w, conv_b.reshape(1, -1), conv_b.reshape(1, -1))


def _rope_tables(t):
    half = A_HEAD_DIM // 2
    inv = jnp.power(ROPE_THETA, -jnp.arange(half, dtype=F32) / half)
    ang = jnp.arange(t, dtype=F32)[:, None] * inv[None, :]
    reps = LANES // half
    return jnp.tile(jnp.cos(ang), (1, reps)), jnp.tile(jnp.sin(ang), (1, reps))


def _layer(x, mem, p, w, layer_idx):
    b, t, d = x.shape
    m = b * t
    lam_init = 0.8 - 0.6 * math.exp(-0.3 * layer_idx)
    x2 = x.reshape(m, d)
    h = rmsnorm_rows(x2, p["g_mix"])
    zqkv = matmul(h, w["w_qkv"], BF16).reshape(b, t, -1)
    zr = matmul(h, w["w_r"], F32).reshape(b, t, -1)
    zm = matmul(h, w["w_m"], BF16).reshape(b, t, -1)
    gates = matmul(h, w["w_g"], BF16, act="sigmoid")
    cos, sin = _rope_tables(t)
    qk = qk_prep(zqkv, cos, sin, p["q_norm"], p["k_norm"])
    o_a = diff_attention(qk, zqkv, p["lambda_q1"], p["lambda_k1"], p["lambda_q2"], p["lambda_k2"],
                         p["attn_subln"], lam_init)
    r, v, kk, lw, kd, bb, g, bonus = rwkv_prep(zr, p["mu_prev"], p["mu_next"], p["decay_w0"], p["decay_w2"],
                                               p["aaa_a0"], p["aaa_a2"], p["gate_g2"], p["k_k"], p["k_a"],
                                               p["r_k"])
    y = rwkv_scan(r, v, kk, lw, kd, bb)
    o_r = rwkv_post(y, bonus, g, p["lnx_w"], p["lnx_b"])
    hm = rmsnorm_rows(mem.reshape(b * N_MEM, d), p["g_memnorm"])
    kv = matmul(hm, w["w_mem_kv"], F32).reshape(b, N_MEM, -1)
    o_m = memory_attention(zm, kv, p["mq_norm"], p["mk_norm"])
    merged = gated_merge(o_a.reshape(m, -1), o_r.reshape(m, -1), o_m.reshape(m, -1),
                         w["p_attn"], w["p_rwkv"], w["p_mem"], gates)
    x2 = matmul_residual(merged, w["w_o"], x2)
    h2 = rmsnorm_rows(x2, p["g_ffn"]).reshape(b, t, d)
    act = ffn_up(h2, w["w_up"], p["conv_w"], p["conv_b"])
    x2 = matmul_residual(act.reshape(m, -1), w["w_down"], x2)
    return x2.reshape(b, t, d)


def _prepare_weights(p):
    w_in = p["w_in"]
    c0 = 3 * A_WIDTH
    c1 = c0 + R_COLS
    c2 = c1 + M_WIDTH
    return {
        "w_qkv": w_in[:, :c0].astype(BF16),
        "w_r": jnp.pad(w_in[:, c0:c1], ((0, 0), (0, R_COLS_PAD - R_COLS))).astype(BF16),
        "w_m": w_in[:, c1:c2].astype(BF16),
        "w_g": w_in[:, c2:].astype(BF16),
        "w_mem_kv": p["w_mem_kv"].astype(BF16),
        "p_attn": p["p_attn"].astype(BF16),
        "p_rwkv": p["p_rwkv"].astype(BF16),
        "p_mem": p["p_mem"].astype(BF16),
        "w_o": p["w_o"].astype(BF16),
        "w_up": p["w_up"].astype(BF16),
        "w_down": p["w_down"].astype(BF16),
    }


def kernel(x_prompt, x_sample, mem_prompt, mem_sample, g_mix, w_in, q_norm, k_norm, lambda_q1, lambda_k1, lambda_q2, lambda_k2, attn_subln, mu_prev, mu_next, decay_w0, decay_w2, aaa_a0, aaa_a2, gate_g2, k_k, k_a, r_k, lnx_w, lnx_b, g_memnorm, w_mem_kv, mq_norm, mk_norm, p_attn, p_rwkv, p_mem, w_o, g_ffn, w_up, conv_w, conv_b, w_down):
    params = dict(g_mix=g_mix, w_in=w_in, q_norm=q_norm, k_norm=k_norm, lambda_q1=lambda_q1,
                  lambda_k1=lambda_k1, lambda_q2=lambda_q2, lambda_k2=lambda_k2, attn_subln=attn_subln,
                  mu_prev=mu_prev, mu_next=mu_next, decay_w0=decay_w0, decay_w2=decay_w2, aaa_a0=aaa_a0,
                  aaa_a2=aaa_a2, gate_g2=gate_g2, k_k=k_k, k_a=k_a, r_k=r_k, lnx_w=lnx_w, lnx_b=lnx_b,
                  g_memnorm=g_memnorm, w_mem_kv=w_mem_kv, mq_norm=mq_norm, mk_norm=mk_norm, p_attn=p_attn,
                  p_rwkv=p_rwkv, p_mem=p_mem, w_o=w_o, g_ffn=g_ffn, w_up=w_up, conv_w=conv_w, conv_b=conv_b,
                  w_down=w_down)
    layers = [{name: arr[l] for name, arr in params.items()} for l in range(g_mix.shape[0])]
    weights = [_prepare_weights(p) for p in layers]
    outs = []
    for x, mem in ((x_prompt, mem_prompt), (x_sample, mem_sample)):
        for l, (p, w) in enumerate(zip(layers, weights)):
            x = _layer(x, mem, p, w, l)
        outs.append(x)
    return tuple(outs)
```

```python
import functools
import math

import jax
import jax.numpy as jnp
from jax import lax
from jax.experimental import pallas as pl
from jax.experimental.pallas import tpu as pltpu

F32 = jnp.float32
BF16 = jnp.bfloat16

D_MODEL = 2048
A_HEADS = 8
A_HEAD_DIM = 64
A_WIDTH = 1024
ROPE_THETA = 10000.0
R_HEAD = 64
R_WIDTH = 1024
R_COLS = 3488
R_COLS_PAD = 3584
GATE_RANK = 160
GATE_RANK_PAD = 256
LNX_EPS = 64e-5
N_MEM = 256
M_HEADS = 4
M_HEAD_DIM = 256
M_WIDTH = 1024
D_FF = 5632
EPS = 1e-6
LANES = 128
CHUNK = 64
LOG2E = 1.4426950408889634

NN = (((1,), (0,)), ((), ()))
NT = (((1,), (1,)), ((), ()))
TN = (((0,), (0,)), ((), ()))


def _params(sem, vmem_mb):
    return pltpu.CompilerParams(dimension_semantics=sem, vmem_limit_bytes=vmem_mb << 20)


def _dot(a, b, dims=NN):
    return lax.dot_general(a, b, dims, preferred_element_type=F32)


def _rmsnorm_kernel(x_ref, g_ref, o_ref):
    x = x_ref[...]
    ms = jnp.mean(x * x, axis=-1, keepdims=True)
    o_ref[...] = (x * lax.rsqrt(ms + EPS) * g_ref[...]).astype(o_ref.dtype)


def rmsnorm_rows(x2, g, bm=512):
    m, d = x2.shape
    bm = min(bm, m)
    return pl.pallas_call(
        _rmsnorm_kernel,
        out_shape=jax.ShapeDtypeStruct((m, d), BF16),
        grid=(m // bm,),
        in_specs=[pl.BlockSpec((bm, d), lambda i: (i, 0)),
                  pl.BlockSpec((1, d), lambda i: (0, 0))],
        out_specs=pl.BlockSpec((bm, d), lambda i: (i, 0)),
        compiler_params=_params(("parallel",), 40),
        name="rmsnorm_rows",
    )(x2, g.reshape(1, d))


def _mm_kernel(a_ref, b_ref, o_ref, *, act):
    acc = _dot(a_ref[...], b_ref[...])
    if act == "sigmoid":
        acc = jax.nn.sigmoid(acc)
    o_ref[...] = acc.astype(o_ref.dtype)


def matmul(a, b, out_dtype, act=None, bm=1024, bn=1024):
    m, k = a.shape
    n = b.shape[1]
    bm, bn = min(bm, m), min(bn, n)
    return pl.pallas_call(
        functools.partial(_mm_kernel, act=act),
        out_shape=jax.ShapeDtypeStruct((m, n), out_dtype),
        grid=(m // bm, n // bn),
        in_specs=[pl.BlockSpec((bm, k), lambda i, j: (i, 0)),
                  pl.BlockSpec((k, bn), lambda i, j: (0, j))],
        out_specs=pl.BlockSpec((bm, bn), lambda i, j: (i, j)),
        compiler_params=_params(("parallel", "arbitrary"), 48),
        name="matmul",
    )(a, b)


def _mm_res_kernel(a_ref, b_ref, r_ref, o_ref):
    o_ref[...] = r_ref[...] + _dot(a_ref[...], b_ref[...])


def matmul_residual(a, b, res, bm=1024, bn=512):
    m, k = a.shape
    n = b.shape[1]
    bm, bn = min(bm, m), min(bn, n)
    return pl.pallas_call(
        _mm_res_kernel,
        out_shape=jax.ShapeDtypeStruct((m, n), F32),
        grid=(m // bm, n // bn),
        in_specs=[pl.BlockSpec((bm, k), lambda i, j: (i, 0)),
                  pl.BlockSpec((k, bn), lambda i, j: (0, j)),
                  pl.BlockSpec((bm, bn), lambda i, j: (i, j))],
        out_specs=pl.BlockSpec((bm, bn), lambda i, j: (i, j)),
        compiler_params=_params(("parallel", "arbitrary"), 56),
        name="matmul_residual",
    )(a, b, res)


def _group_ones(group):
    r = lax.broadcasted_iota(jnp.int32, (2 * LANES, LANES), 0)
    c = lax.broadcasted_iota(jnp.int32, (2 * LANES, LANES), 1)
    return jnp.where((r % LANES) // group == c // group, 1.0, 0.0).astype(BF16)


def _group_sum(x, ones):
    hi = x.astype(BF16)
    lo = (x - hi.astype(F32)).astype(BF16)
    outs = []
    for c in range(x.shape[1] // LANES):
        sl = slice(LANES * c, LANES * (c + 1))
        outs.append(_dot(jnp.concatenate([hi[:, sl], lo[:, sl]], axis=1), ones))
    return outs[0] if len(outs) == 1 else jnp.concatenate(outs, axis=1)


def _tile_lanes(x, reps):
    return x if reps == 1 else jnp.concatenate([x] * reps, axis=1)


def _shift_rows(u, prev_row, next_row):
    n = u.shape[0]
    row = lax.broadcasted_iota(jnp.int32, u.shape, 0)
    up = jnp.where(row == 0, prev_row, pltpu.roll(u, 1, 0))
    dn = jnp.where(row == n - 1, next_row, pltpu.roll(u, n - 1, 0))
    return up, dn


def _qk_prep_kernel(x_ref, cos_ref, sin_ref, gq_ref, gk_ref, o_ref):
    c = pl.program_id(2)
    x = x_ref[0].astype(F32)
    reps = x.shape[1] // LANES
    ss = _group_sum(x * x, _group_ones(A_HEAD_DIM))
    g = jnp.where(c == 0, gq_ref[...], gk_ref[...])
    xn = x * lax.rsqrt(ss * (1.0 / A_HEAD_DIM) + EPS) * _tile_lanes(g, reps)
    lane = lax.broadcasted_iota(jnp.int32, x.shape, 1)
    half = A_HEAD_DIM // 2
    rot = jnp.where(lane % A_HEAD_DIM < half,
                    -pltpu.roll(xn, x.shape[1] - half, 1), pltpu.roll(xn, half, 1))
    y = xn * _tile_lanes(cos_ref[...], reps) + rot * _tile_lanes(sin_ref[...], reps)
    scale = jnp.where(c == 0, A_HEAD_DIM ** -0.5 * LOG2E, 1.0)
    o_ref[0] = (y * scale).astype(o_ref.dtype)


def qk_prep(zqkv, cos, sin, q_norm, k_norm, bt=512):
    b, t, _ = zqkv.shape
    bt = min(bt, t)
    gq = jnp.tile(q_norm.reshape(1, A_HEAD_DIM), (1, LANES // A_HEAD_DIM))
    gk = jnp.tile(k_norm.reshape(1, A_HEAD_DIM), (1, LANES // A_HEAD_DIM))
    return pl.pallas_call(
        _qk_prep_kernel,
        out_shape=jax.ShapeDtypeStruct((b, t, 2 * A_WIDTH), BF16),
        grid=(b, t // bt, 2),
        in_specs=[pl.BlockSpec((1, bt, A_WIDTH), lambda i, j, c: (i, j, c)),
                  pl.BlockSpec((bt, LANES), lambda i, j, c: (j, 0)),
                  pl.BlockSpec((bt, LANES), lambda i, j, c: (j, 0)),
                  pl.BlockSpec((1, LANES), lambda i, j, c: (0, 0)),
                  pl.BlockSpec((1, LANES), lambda i, j, c: (0, 0))],
        out_specs=pl.BlockSpec((1, bt, A_WIDTH), lambda i, j, c: (i, j, c)),
        compiler_params=_params(("parallel", "parallel", "arbitrary"), 40),
        name="qk_prep",
    )(zqkv, cos, sin, gq, gk)


def _dattn_kernel(lq1_ref, lk1_ref, lq2_ref, lk2_ref, g_ref, q_ref, k_ref, v_ref, o_ref,
                  m_ref, acc_ref, s_ref, *, bk, group, lam_init):
    bq = q_ref.shape[1]
    nk = k_ref.shape[1] // bk
    q = q_ref[0]
    lane = lax.broadcasted_iota(jnp.int32, q.shape, 1)
    zero = jnp.zeros_like(q)
    first = lane < A_HEAD_DIM
    qq = jnp.concatenate([jnp.where(first, q, zero), jnp.where(first, zero, q)], axis=0)
    m_ref[...] = jnp.full(m_ref.shape, -jnp.inf, F32)
    acc_ref[...] = jnp.zeros(acc_ref.shape, F32)
    ones = jnp.ones((bk, LANES), BF16)

    def scores(j):
        start = pl.multiple_of(j * bk, bk)
        return _dot(qq, k_ref[0, pl.ds(start, bk), :], NT)

    def accumulate(j, s):
        start = pl.multiple_of(j * bk, bk)
        vs = v_ref[0, pl.ds(start, bk), :]
        m_prev = m_ref[...]
        m_new = jnp.maximum(m_prev, jnp.max(s, axis=-1, keepdims=True))
        alpha = jnp.exp2(m_prev - m_new)
        p = jnp.exp2(s - _tile_lanes(m_new, bk // LANES))
        pv = _dot(p.astype(BF16), jnp.concatenate([vs, ones], axis=1))
        acc_ref[...] = _tile_lanes(alpha, 2) * acc_ref[...] + pv
        m_ref[...] = m_new

    def run_group(j0, prefetch_last):
        for u in range(group):
            if u < group - 1 or prefetch_last:
                s_ref[(u + 1) % 2] = scores(j0 + u + 1)
            accumulate(j0 + u, s_ref[u % 2])

    s_ref[0] = scores(0)

    def body(i, carry):
        run_group(group * i, True)
        return carry

    lax.fori_loop(0, nk // group - 1, body, 0)
    run_group(nk - group, False)
    lam = (jnp.exp(jnp.sum(lq1_ref[...] * lk1_ref[...], keepdims=True))
           - jnp.exp(jnp.sum(lq2_ref[...] * lk2_ref[...], keepdims=True)) + lam_init)
    o12 = acc_ref[:, :LANES] / acc_ref[:, LANES:]
    o = o12[:bq] - lam * o12[bq:]
    ms = jnp.mean(o * o, axis=-1, keepdims=True)
    y = (o * lax.rsqrt(ms + EPS) * g_ref[...]) * (1.0 - lam_init)
    o_ref[0] = y.astype(o_ref.dtype)


def diff_attention(qk, zqkv, lq1, lk1, lq2, lk2, sub_g, lam_init, bq=512, bk=512):
    b, t, _ = qk.shape
    group = 4
    bq, bk = min(bq, t), min(bk, t // group)
    assert (t // bk) % group == 0
    vec = lambda a: a.reshape(1, -1)
    small = lambda n: pl.BlockSpec((1, n), lambda i, h, j: (0, 0))
    return pl.pallas_call(
        functools.partial(_dattn_kernel, bk=bk, group=group, lam_init=lam_init),
        out_shape=jax.ShapeDtypeStruct((b, t, A_WIDTH), BF16),
        grid=(b, A_HEADS, t // bq),
        in_specs=[small(A_HEAD_DIM), small(A_HEAD_DIM), small(A_HEAD_DIM), small(A_HEAD_DIM),
                  small(LANES),
                  pl.BlockSpec((1, bq, LANES), lambda i, h, j: (i, j, h)),
                  pl.BlockSpec((1, t, LANES), lambda i, h, j: (i, 0, A_HEADS + h)),
                  pl.BlockSpec((1, t, LANES), lambda i, h, j: (i, 0, 2 * A_HEADS + h))],
        out_specs=pl.BlockSpec((1, bq, LANES), lambda i, h, j: (i, j, h)),
        scratch_shapes=[pltpu.VMEM((2 * bq, LANES), F32), pltpu.VMEM((2 * bq, 2 * LANES), F32),
                        pltpu.VMEM((2, 2 * bq, bk), F32)],
        compiler_params=_params(("parallel", "parallel", "arbitrary"), 48),
        name="diff_attention",
    )(vec(lq1), vec(lk1), vec(lq2), vec(lk2), vec(sub_g), qk, qk, zqkv)


def _mem_attn_kernel(q_ref, kv_ref, gq_ref, gk_ref, o_ref):
    outs = []
    for h in range(M_HEADS):
        sl = slice(h * M_HEAD_DIM, (h + 1) * M_HEAD_DIM)
        qh = q_ref[0, :, sl].astype(F32)
        qn = qh * lax.rsqrt(jnp.mean(qh * qh, axis=-1, keepdims=True) + EPS) * gq_ref[...]
        kh = kv_ref[0, :, sl]
        kn = kh * lax.rsqrt(jnp.mean(kh * kh, axis=-1, keepdims=True) + EPS) * gk_ref[...]
        vh = kv_ref[0, :, M_WIDTH + h * M_HEAD_DIM:M_WIDTH + (h + 1) * M_HEAD_DIM]
        s = _dot((qn * M_HEAD_DIM ** -0.5).astype(BF16), kn.astype(BF16), NT)
        p = jnp.exp(s - jnp.max(s, axis=-1, keepdims=True))
        o = _dot(p.astype(BF16), vh.astype(BF16)) / jnp.sum(p, axis=-1, keepdims=True)
        outs.append(o)
    o_ref[0] = jnp.concatenate(outs, axis=1).astype(o_ref.dtype)


def memory_attention(zm, kv, mq_norm, mk_norm, bt=512):
    b, t, _ = zm.shape
    bt = min(bt, t)
    return pl.pallas_call(
        _mem_attn_kernel,
        out_shape=jax.ShapeDtypeStruct((b, t, M_WIDTH), BF16),
        grid=(b, t // bt),
        in_specs=[pl.BlockSpec((1, bt, M_WIDTH), lambda i, j: (i, j, 0)),
                  pl.BlockSpec((1, N_MEM, 2 * M_WIDTH), lambda i, j: (i, 0, 0)),
                  pl.BlockSpec((1, M_HEAD_DIM), lambda i, j: (0, 0)),
                  pl.BlockSpec((1, M_HEAD_DIM), lambda i, j: (0, 0))],
        out_specs=pl.BlockSpec((1, bt, M_WIDTH), lambda i, j: (i, j, 0)),
        compiler_params=_params(("parallel", "arbitrary"), 40),
        name="memory_attention",
    )(zm, kv, mq_norm.reshape(1, -1), mk_norm.reshape(1, -1))


def _rwkv_prep_kernel(z_ref, zp_ref, zn_ref, mup_ref, mun_ref, w0_ref, w2_ref, a0_ref, a2_ref, g2_ref,
                      kk_w_ref, ka_ref, rk_ref,
                      r_ref, v_ref, kk_ref, lw_ref, kd_ref, bb_ref, g_ref, bonus_ref):
    j = pl.program_id(1)
    nt = pl.num_programs(1)
    z = z_ref[0]
    prev_row = jnp.where(j > 0, zp_ref[0, 7:8, :], 0.0)
    next_row = jnp.where(j < nt - 1, zn_ref[0, 0:1, :], 0.0)
    zp, zn = _shift_rows(z, prev_row, next_row)
    z = z + mup_ref[...] * (zp - z) + mun_ref[...] * (zn - z)
    w = R_WIDTH
    r, k, v = z[:, 0:w], z[:, w:2 * w], z[:, 2 * w:3 * w]
    zw = z[:, 3 * w:3 * w + LANES]
    za = z[:, 3 * w + LANES:3 * w + 2 * LANES]
    zg = z[:, 3 * w + 2 * LANES:]
    wlin = w0_ref[...] + _dot(jnp.tanh(zw).astype(BF16), w2_ref[...])
    neg = -wlin
    softplus = jnp.maximum(neg, 0.0) + jnp.log(1.0 + jnp.exp(-jnp.abs(neg)))
    lw = -jnp.exp(-softplus - 0.5)
    a = jax.nn.sigmoid(a0_ref[...] + _dot(za.astype(BF16), a2_ref[...]))
    g = _dot(jax.nn.sigmoid(zg).astype(BF16), g2_ref[...])
    ones = _group_ones(R_HEAD)
    kk = k * kk_w_ref[...]
    norm = jnp.sqrt(_group_sum(kk * kk, ones))
    kk = kk / jnp.maximum(norm, 1e-12)
    ka = ka_ref[...]
    kd0 = k * (1.0 + (a[:, :w] - 1.0) * ka)
    kd1 = k * (1.0 + (a[:, w:] - 1.0) * ka)
    bonus = _group_sum(r * (kd0 + kd1) * rk_ref[...], ones) * v
    r_ref[0] = r
    v_ref[0] = v
    kk_ref[0] = kk
    lw_ref[0, 0] = lw[:, :w]
    lw_ref[0, 1] = lw[:, w:]
    kd_ref[0, 0] = kd0
    kd_ref[0, 1] = kd1
    bb_ref[0, 0] = kk * a[:, :w]
    bb_ref[0, 1] = kk * a[:, w:]
    g_ref[0] = g
    bonus_ref[0] = bonus


def rwkv_prep(zr, mu_prev, mu_next, decay_w0, decay_w2, aaa_a0, aaa_a2, gate_g2, k_k, k_a, r_k, bt=256):
    b, t, cols = zr.shape
    bt = min(bt, t)
    w = R_WIDTH
    pad = cols - R_COLS
    mup = jnp.pad(mu_prev, (0, pad)).reshape(1, cols)
    mun = jnp.pad(mu_next, (0, pad)).reshape(1, cols)

    def blockdiag(m):
        z = jnp.zeros_like(m[0])
        return jnp.concatenate([jnp.concatenate([m[0], z], axis=1),
                                jnp.concatenate([z, m[1]], axis=1)], axis=0).astype(BF16)

    g2 = jnp.pad(gate_g2, ((0, GATE_RANK_PAD - GATE_RANK), (0, 0))).astype(BF16)
    row = lambda n: pl.BlockSpec((1, n), lambda i, j: (0, 0))
    full = lambda r_, c_: pl.BlockSpec((r_, c_), lambda i, j: (0, 0))
    hb = bt // 8
    nh = t // 8
    tok = pl.BlockSpec((1, bt, w), lambda i, j: (i, j, 0))
    tok2 = pl.BlockSpec((1, 2, bt, w), lambda i, j: (i, 0, j, 0))
    s1 = jax.ShapeDtypeStruct((b, t, w), F32)
    s2 = jax.ShapeDtypeStruct((b, 2, t, w), F32)
    return pl.pallas_call(
        _rwkv_prep_kernel,
        out_shape=(s1, s1, s1, s2, s2, s2, s1, s1),
        grid=(b, t // bt),
        in_specs=[pl.BlockSpec((1, bt, cols), lambda i, j: (i, j, 0)),
                  pl.BlockSpec((1, 8, cols), lambda i, j: (i, jnp.maximum(j * hb - 1, 0), 0)),
                  pl.BlockSpec((1, 8, cols), lambda i, j: (i, jnp.minimum((j + 1) * hb, nh - 1), 0)),
                  row(cols), row(cols), row(2 * w), full(LANES, 2 * w), row(2 * w), full(LANES, 2 * w),
                  full(GATE_RANK_PAD, w), row(w), row(w), row(w)],
        out_specs=(tok, tok, tok, tok2, tok2, tok2, tok, tok),
        compiler_params=_params(("parallel", "arbitrary"), 56),
        name="rwkv_prep",
    )(zr, zr, zr, mup, mun, decay_w0.reshape(1, 2 * w), blockdiag(decay_w2), aaa_a0.reshape(1, 2 * w),
      blockdiag(aaa_a2), g2, k_k.reshape(1, w), k_a.reshape(1, w), r_k.reshape(1, w))


def _split3(x):
    hi = x.astype(BF16)
    r1 = x - hi.astype(F32)
    mid = r1.astype(BF16)
    lo = (r1 - mid.astype(F32)).astype(BF16)
    return hi, mid, lo


def _rwkv_scan_kernel(rf_ref, vf_ref, kkf_ref, lwf_ref, kdf_ref, bbf_ref,
                      rb_ref, vb_ref, kkb_ref, lwb_ref, kdb_ref, bbb_ref, yf_ref, yb_ref, s_ref):
    C = CHUNK
    n_pairs = R_WIDTH // LANES
    assert C == R_HEAD

    @pl.when(pl.program_id(1) == 0)
    def _():
        s_ref[...] = jnp.zeros(s_ref.shape, F32)

    ti = lax.broadcasted_iota(jnp.int32, (C, C), 0)
    si = lax.broadcasted_iota(jnp.int32, (C, C), 1)
    r2 = lax.broadcasted_iota(jnp.int32, (2 * C, 2 * C), 0)
    c2 = lax.broadcasted_iota(jnp.int32, (2 * C, 2 * C), 1)
    same = (r2 // C) == (c2 // C)
    lane = lax.broadcasted_iota(jnp.int32, (C, LANES), 1)
    row = lax.broadcasted_iota(jnp.int32, (C, LANES), 0)
    first = lane < R_HEAD
    src = lane % C
    eye_h = jnp.where(row == src, 1.0, 0.0)

    def stack(x):
        return jnp.concatenate([jnp.where(first, x, 0.0), jnp.where(first, 0.0, x)], axis=0)

    chains = []
    for d, (r_ref, v_ref, kk_ref, lw_ref, kd_ref, bb_ref, y_ref) in enumerate(
            ((rf_ref, vf_ref, kkf_ref, lwf_ref, kdf_ref, bbf_ref, yf_ref),
             (rb_ref, vb_ref, kkb_ref, lwb_ref, kdb_ref, bbb_ref, yb_ref))):
        sign = 1 if d == 0 else -1
        delta = sign * (r2 - c2)
        strict = same & (delta > 0)
        incl = same & (delta >= 0)
        strict_h = sign * (row - src) > 0
        tri = jnp.where(sign * (ti - si) >= 0, 1.0, 0.0).astype(BF16)
        lw_all = lw_ref[0, 0]
        hi, mid, lo = _split3(lw_all)
        cum_all = _dot(tri, hi) + (_dot(tri, mid) + _dot(tri, lo))
        tot_all = jnp.sum(lw_all, axis=0, keepdims=True)
        for p in range(n_pairs):
            sl = slice(p * LANES, (p + 1) * LANES)
            chains.append(dict(d=d, p=p, sl=sl, strict=strict, incl=incl, strict_h=strict_h,
                               lw=lw_all[:, sl], cum=cum_all[:, sl], tot=tot_all[:, sl],
                               r=r_ref, v=v_ref, kk=kk_ref, kd=kd_ref, bb=bb_ref, y=y_ref))

    for ch in chains:
        sl, lw, cum, tot = ch["sl"], ch["lw"], ch["cum"], ch["tot"]
        r, v, kk = ch["r"][0, :, sl], ch["v"][0, :, sl], ch["kk"][0, :, sl]
        kd, bb = ch["kd"][0, 0, :, sl], ch["bb"][0, 0, :, sl]
        e_neg = jnp.exp(-cum)
        e_end = jnp.exp(tot - cum)
        at = -kk * jnp.exp(cum - lw)
        rt = r * jnp.exp(cum)
        ch["ar"] = jnp.concatenate([stack(at), stack(rt)], axis=0).astype(BF16)
        ch["bk"] = jnp.concatenate([stack(bb * e_neg), stack(kd * e_neg)], axis=0).astype(BF16)
        ch["bk_end"] = jnp.concatenate([stack(bb * e_end), stack(kd * e_end)], axis=0).astype(BF16)
        ch["vs"] = stack(v).astype(BF16)
    for ch in chains:
        ch["pm"] = _dot(ch["ar"], ch["bk"], NT)
    for ch in chains:
        pm = ch["pm"]
        ch["lak_mrk"] = jnp.concatenate([jnp.where(ch["strict"], pm[:2 * C, 2 * C:], 0.0),
                                         jnp.where(ch["incl"], pm[2 * C:, 2 * C:], 0.0)],
                                        axis=0).astype(BF16)
        ch["mrb"] = jnp.where(ch["incl"], pm[2 * C:, :2 * C], 0.0).astype(BF16)
        ch["lp"] = jnp.where(ch["strict_h"], jnp.where(first, pm[:C, :2 * C], pm[C:2 * C, :2 * C]), 0.0)
        ch["tm"] = eye_h + ch["lp"]
    n = 1
    while 2 * n < C:
        for ch in chains:
            ch["lp"] = _dot(ch["lp"].astype(BF16), stack(ch["lp"]).astype(BF16))
        for ch in chains:
            ch["tm"] = ch["tm"] + _dot(ch["tm"].astype(BF16), stack(ch["lp"]).astype(BF16))
        n *= 2
    for ch in chains:
        ch["s"] = s_ref[ch["d"], ch["p"]]
        ch["a_s"] = _dot(ch["ar"], ch["s"].astype(BF16), NT)
    for ch in chains:
        ch["lv"] = _dot(ch["lak_mrk"], ch["vs"])
    for ch in chains:
        rhs = (ch["a_s"][:2 * C] + ch["lv"][:2 * C]).astype(BF16)
        ch["ub"] = _dot(stack(ch["tm"]).astype(BF16), rhs).astype(BF16)
    for ch in chains:
        yy = ch["a_s"][2 * C:] + ch["lv"][2 * C:] + _dot(ch["mrb"], ch["ub"])
        ch["y"][0, :, ch["sl"]] = yy[:C] + yy[C:]
    for ch in chains:
        uv = jnp.concatenate([ch["ub"], ch["vs"]], axis=0)
        s_ref[ch["d"], ch["p"]] = ch["s"] * jnp.exp(ch["tot"]) + _dot(uv, ch["bk_end"], TN)


def rwkv_scan(r, v, kk, lw, kd, bb):
    b, t, w = r.shape
    nc = t // CHUNK
    fwd = pl.BlockSpec((1, CHUNK, w), lambda i, c: (i, c, 0))
    bwd = pl.BlockSpec((1, CHUNK, w), lambda i, c: (i, nc - 1 - c, 0))
    fwd2 = pl.BlockSpec((1, 1, CHUNK, w), lambda i, c: (i, 0, c, 0))
    bwd2 = pl.BlockSpec((1, 1, CHUNK, w), lambda i, c: (i, 1, nc - 1 - c, 0))
    out = jax.ShapeDtypeStruct((b, t, w), F32)
    return pl.pallas_call(
        _rwkv_scan_kernel,
        out_shape=(out, out),
        grid=(b, nc),
        in_specs=[fwd, fwd, fwd, fwd2, fwd2, fwd2, bwd, bwd, bwd, bwd2, bwd2, bwd2],
        out_specs=(fwd, bwd),
        scratch_shapes=[pltpu.VMEM((2, w // LANES, LANES, LANES), F32)],
        compiler_params=_params(("parallel", "arbitrary"), 48),
        name="rwkv_scan",
    )(r, v, kk, lw, kd, bb, r, v, kk, lw, kd, bb)


def _rwkv_post_kernel(yf_ref, yb_ref, bonus_ref, g_ref, w_ref, b_ref, o_ref):
    y = yf_ref[0] + yb_ref[0]
    ones = _group_ones(R_HEAD)
    mu = _group_sum(y, ones) * (1.0 / R_HEAD)
    dlt = y - mu
    var = _group_sum(dlt * dlt, ones) * (1.0 / R_HEAD)
    yn = dlt * lax.rsqrt(var + LNX_EPS) * w_ref[...] + b_ref[...]
    o_ref[0] = ((yn + bonus_ref[0]) * g_ref[0]).astype(o_ref.dtype)


def rwkv_post(y_fwd, y_bwd, bonus, g, lnx_w, lnx_b, bt=512):
    b, t, w = y_fwd.shape
    bt = min(bt, t)
    tok = pl.BlockSpec((1, bt, w), lambda i, j: (i, j, 0))
    row = pl.BlockSpec((1, w), lambda i, j: (0, 0))
    return pl.pallas_call(
        _rwkv_post_kernel,
        out_shape=jax.ShapeDtypeStruct((b, t, w), BF16),
        grid=(b, t // bt),
        in_specs=[tok, tok, tok, tok, row, row],
        out_specs=tok,
        compiler_params=_params(("parallel", "arbitrary"), 40),
        name="rwkv_post",
    )(y_fwd, y_bwd, bonus, g, lnx_w.reshape(1, w), lnx_b.reshape(1, w))


def _merge_kernel(oa_ref, or_ref, om_ref, pa_ref, pr_ref, pm_ref, ga_ref, gr_ref, gm_ref, o_ref):
    acc = ga_ref[...].astype(F32) * _dot(oa_ref[...], pa_ref[...])
    acc += gr_ref[...].astype(F32) * _dot(or_ref[...], pr_ref[...])
    acc += gm_ref[...].astype(F32) * _dot(om_ref[...], pm_ref[...])
    o_ref[...] = acc.astype(o_ref.dtype)


def gated_merge(o_a, o_r, o_m, p_a, p_r, p_m, gates, bm=1024, bn=1024):
    m, k = o_a.shape
    n = p_a.shape[1]
    bm, bn = min(bm, m), min(bn, n)
    nb = n // bn
    a_spec = pl.BlockSpec((bm, k), lambda i, j: (i, 0))
    w_spec = pl.BlockSpec((k, bn), lambda i, j: (0, j))
    gate = lambda o: pl.BlockSpec((bm, bn), lambda i, j: (i, j + o * nb))
    return pl.pallas_call(
        _merge_kernel,
        out_shape=jax.ShapeDtypeStruct((m, n), BF16),
        grid=(m // bm, nb),
        in_specs=[a_spec, a_spec, a_spec, w_spec, w_spec, w_spec, gate(0), gate(1), gate(2)],
        out_specs=pl.BlockSpec((bm, bn), lambda i, j: (i, j)),
        compiler_params=_params(("parallel", "arbitrary"), 56),
        name="gated_merge",
    )(o_a, o_r, o_m, p_a, p_r, p_m, gates, gates, gates)


def _ffn_up_kernel(h_ref, hp_ref, hn_ref, wg_ref, wv_ref, cg_ref, cv_ref, bg_ref, bv_ref, o_ref, lhs_ref):
    j = pl.program_id(1)
    nt = pl.num_programs(1)
    bm = h_ref.shape[1]
    halo = hp_ref.shape[1]

    @pl.when(pl.program_id(2) == 0)
    def _():
        lhs_ref[0:bm] = h_ref[0]
        lhs_ref[bm:bm + halo] = hp_ref[0]
        lhs_ref[bm + halo:] = hn_ref[0]

    def branch(w_ref, c_ref, b_ref):
        u_all = _dot(lhs_ref[...], w_ref[...])
        u = u_all[:bm]
        prev_row = jnp.where(j > 0, u_all[bm + halo - 1:bm + halo], 0.0)
        next_row = jnp.where(j < nt - 1, u_all[bm + halo:bm + halo + 1], 0.0)
        up, dn = _shift_rows(u, prev_row, next_row)
        c = c_ref[...]
        return c[0:1] * up + c[1:2] * u + c[2:3] * dn + b_ref[...]

    gt = branch(wg_ref, cg_ref, bg_ref)
    val = branch(wv_ref, cv_ref, bv_ref)
    o_ref[0] = (gt * jax.nn.sigmoid(gt) * val).astype(o_ref.dtype)


def ffn_up(h, w_up, conv_w, conv_b, bm=1024, bn=512):
    b, t, k = h.shape
    bm = min(bm, t)
    nb = D_FF // bn
    halo = 16
    hb = bm // halo
    nh = t // halo
    return pl.pallas_call(
        _ffn_up_kernel,
        out_shape=jax.ShapeDtypeStruct((b, t, D_FF), BF16),
        grid=(b, t // bm, nb),
        in_specs=[pl.BlockSpec((1, bm, k), lambda i, j, n: (i, j, 0)),
                  pl.BlockSpec((1, halo, k), lambda i, j, n: (i, jnp.maximum(j * hb - 1, 0), 0)),
                  pl.BlockSpec((1, halo, k), lambda i, j, n: (i, jnp.minimum((j + 1) * hb, nh - 1), 0)),
                  pl.BlockSpec((k, bn), lambda i, j, n: (0, n)),
                  pl.BlockSpec((k, bn), lambda i, j, n: (0, n + nb)),
                  pl.BlockSpec((3, bn), lambda i, j, n: (0, n)),
                  pl.BlockSpec((3, bn), lambda i, j, n: (0, n + nb)),
                  pl.BlockSpec((1, bn), lambda i, j, n: (0, n)),
                  pl.BlockSpec((1, bn), lambda i, j, n: (0, n + nb))],
        out_specs=pl.BlockSpec((1, bm, bn), lambda i, j, n: (i, j, n)),
        scratch_shapes=[pltpu.VMEM((bm + 2 * halo, k), BF16)],
        compiler_params=_params(("parallel", "parallel", "arbitrary"), 56),
        name="ffn_up",
    )(h, h, h, w_up, w_up, conv_w, conv_w, conv_b.reshape(1, -1), conv_b.reshape(1, -1))


def _rope_tables(t):
    half = A_HEAD_DIM // 2
    inv = jnp.power(ROPE_THETA, -jnp.arange(half, dtype=F32) / half)
    ang = jnp.arange(t, dtype=F32)[:, None] * inv[None, :]
    reps = LANES // half
    return jnp.tile(jnp.cos(ang), (1, reps)), jnp.tile(jnp.sin(ang), (1, reps))


def _layer(x, mem, p, w, layer_idx):
    b, t, d = x.shape
    m = b * t
    lam_init = 0.8 - 0.6 * math.exp(-0.3 * layer_idx)
    x2 = x.reshape(m, d)
    h = rmsnorm_rows(x2, p["g_mix"])
    zr = matmul(h, w["w_r"], F32, bn=896).reshape(b, t, -1)
    zm = matmul(h, w["w_m"], BF16).reshape(b, t, -1)
    zqkv = matmul(h, w["w_qkv"], BF16).reshape(b, t, -1)
    gates = matmul(h, w["w_g"], BF16, act="sigmoid")
    r, v, kk, lw, kd, bb, g, bonus = rwkv_prep(zr, p["mu_prev"], p["mu_next"], p["decay_w0"], p["decay_w2"],
                                               p["aaa_a0"], p["aaa_a2"], p["gate_g2"], p["k_k"], p["k_a"],
                                               p["r_k"])
    y_fwd, y_bwd = rwkv_scan(r, v, kk, lw, kd, bb)
    o_r = rwkv_post(y_fwd, y_bwd, bonus, g, p["lnx_w"], p["lnx_b"])
    hm = rmsnorm_rows(mem.reshape(b * N_MEM, d), p["g_memnorm"])
    kv = matmul(hm, w["w_mem_kv"], F32).reshape(b, N_MEM, -1)
    o_m = memory_attention(zm, kv, p["mq_norm"], p["mk_norm"])
    cos, sin = _rope_tables(t)
    qk = qk_prep(zqkv, cos, sin, p["q_norm"], p["k_norm"])
    o_a = diff_attention(qk, zqkv, p["lambda_q1"], p["lambda_k1"], p["lambda_q2"], p["lambda_k2"],
                         p["attn_subln"], lam_init)
    merged = gated_merge(o_a.reshape(m, -1), o_r.reshape(m, -1), o_m.reshape(m, -1),
                         w["p_attn"], w["p_rwkv"], w["p_mem"], gates)
    x2 = matmul_residual(merged, w["w_o"], x2, bn=1024)
    h2 = rmsnorm_rows(x2, p["g_ffn"]).reshape(b, t, d)
    act = ffn_up(h2, w["w_up"], p["conv_w"], p["conv_b"])
    x2 = matmul_residual(act.reshape(m, -1), w["w_down"], x2)
    return x2.reshape(b, t, d)


def _prepare_weights(p):
    w_in = p["w_in"]
    c0 = 3 * A_WIDTH
    c1 = c0 + R_COLS
    c2 = c1 + M_WIDTH
    return {
        "w_qkv": w_in[:, :c0].astype(BF16),
        "w_r": jnp.pad(w_in[:, c0:c1], ((0, 0), (0, R_COLS_PAD - R_COLS))).astype(BF16),
        "w_m": w_in[:, c1:c2].astype(BF16),
        "w_g": w_in[:, c2:].astype(BF16),
        "w_mem_kv": p["w_mem_kv"].astype(BF16),
        "p_attn": p["p_attn"].astype(BF16),
        "p_rwkv": p["p_rwkv"].astype(BF16),
        "p_mem": p["p_mem"].astype(BF16),
        "w_o": p["w_o"].astype(BF16),
        "w_up": p["w_up"].astype(BF16),
        "w_down": p["w_down"].astype(BF16),
    }


def kernel(x_prompt, x_sample, mem_prompt, mem_sample, g_mix, w_in, q_norm, k_norm, lambda_q1, lambda_k1, lambda_q2, lambda_k2, attn_subln, mu_prev, mu_next, decay_w0, decay_w2, aaa_a0, aaa_a2, gate_g2, k_k, k_a, r_k, lnx_w, lnx_b, g_memnorm, w_mem_kv, mq_norm, mk_norm, p_attn, p_rwkv, p_mem, w_o, g_ffn, w_up, conv_w, conv_b, w_down):
    params = dict(g_mix=g_mix, w_in=w_in, q_norm=q_norm, k_norm=k_norm, lambda_q1=lambda_q1,
                  lambda_k1=lambda_k1, lambda_q2=lambda_q2, lambda_k2=lambda_k2, attn_subln=attn_subln,
                  mu_prev=mu_prev, mu_next=mu_next, decay_w0=decay_w0, decay_w2=decay_w2, aaa_a0=aaa_a0,
                  aaa_a2=aaa_a2, gate_g2=gate_g2, k_k=k_k, k_a=k_a, r_k=r_k, lnx_w=lnx_w, lnx_b=lnx_b,
                  g_memnorm=g_memnorm, w_mem_kv=w_mem_kv, mq_norm=mq_norm, mk_norm=mk_norm, p_attn=p_attn,
                  p_rwkv=p_rwkv, p_mem=p_mem, w_o=w_o, g_ffn=g_ffn, w_up=w_up, conv_w=conv_w, conv_b=conv_b,
                  w_down=w_down)
    layers = [{name: arr[l] for name, arr in params.items()} for l in range(g_mix.shape[0])]
    weights = [_prepare_weights(p) for p in layers]
    outs = []
    for x, mem in ((x_prompt, mem_prompt), (x_sample, mem_sample)):
        for l, (p, w) in enumerate(zip(layers, weights)):
            x = _layer(x, mem, p, w, l)
        outs.append(x)
    return tuple(outs)
```

```python
import functools
import math

import jax
import jax.numpy as jnp
from jax import lax
from jax.experimental import pallas as pl
from jax.experimental.pallas import tpu as pltpu

F32 = jnp.float32
BF16 = jnp.bfloat16

D_MODEL = 2048
A_HEADS = 8
A_HEAD_DIM = 64
A_WIDTH = 1024
ROPE_THETA = 10000.0
R_HEAD = 64
R_WIDTH = 1024
R_COLS = 3488
R_COLS_PAD = 3584
GATE_RANK = 160
GATE_RANK_PAD = 256
LNX_EPS = 64e-5
N_MEM = 256
M_HEADS = 4
M_HEAD_DIM = 256
M_WIDTH = 1024
D_FF = 5632
EPS = 1e-6
LANES = 128
CHUNK = 64
LOG2E = 1.4426950408889634

NN = (((1,), (0,)), ((), ()))
NT = (((1,), (1,)), ((), ()))
TN = (((0,), (0,)), ((), ()))


def _params(sem, vmem_mb):
    return pltpu.CompilerParams(dimension_semantics=sem, vmem_limit_bytes=vmem_mb << 20)


def _dot(a, b, dims=NN):
    return lax.dot_general(a, b, dims, preferred_element_type=F32)


def _rmsnorm_kernel(x_ref, g_ref, o_ref):
    x = x_ref[...]
    ms = jnp.mean(x * x, axis=-1, keepdims=True)
    o_ref[...] = (x * lax.rsqrt(ms + EPS) * g_ref[...]).astype(o_ref.dtype)


def rmsnorm_rows(x2, g, bm=512):
    m, d = x2.shape
    bm = min(bm, m)
    return pl.pallas_call(
        _rmsnorm_kernel,
        out_shape=jax.ShapeDtypeStruct((m, d), BF16),
        grid=(m // bm,),
        in_specs=[pl.BlockSpec((bm, d), lambda i: (i, 0)),
                  pl.BlockSpec((1, d), lambda i: (0, 0))],
        out_specs=pl.BlockSpec((bm, d), lambda i: (i, 0)),
        compiler_params=_params(("parallel",), 40),
        name="rmsnorm_rows",
    )(x2, g.reshape(1, d))


def _mm_kernel(a_ref, b_ref, o_ref, *, act):
    acc = _dot(a_ref[...], b_ref[...])
    if act == "sigmoid":
        acc = jax.nn.sigmoid(acc)
    o_ref[...] = acc.astype(o_ref.dtype)


def matmul(a, b, out_dtype, act=None, bm=1024, bn=1024):
    m, k = a.shape
    n = b.shape[1]
    bm, bn = min(bm, m), min(bn, n)
    return pl.pallas_call(
        functools.partial(_mm_kernel, act=act),
        out_shape=jax.ShapeDtypeStruct((m, n), out_dtype),
        grid=(m // bm, n // bn),
        in_specs=[pl.BlockSpec((bm, k), lambda i, j: (i, 0)),
                  pl.BlockSpec((k, bn), lambda i, j: (0, j))],
        out_specs=pl.BlockSpec((bm, bn), lambda i, j: (i, j)),
        compiler_params=_params(("parallel", "arbitrary"), 48),
        name="matmul",
    )(a, b)


def _mm_res_kernel(a_ref, b_ref, r_ref, o_ref):
    o_ref[...] = r_ref[...] + _dot(a_ref[...], b_ref[...])


def matmul_residual(a, b, res, bm=1024, bn=512):
    m, k = a.shape
    n = b.shape[1]
    bm, bn = min(bm, m), min(bn, n)
    return pl.pallas_call(
        _mm_res_kernel,
        out_shape=jax.ShapeDtypeStruct((m, n), F32),
        grid=(m // bm, n // bn),
        in_specs=[pl.BlockSpec((bm, k), lambda i, j: (i, 0)),
                  pl.BlockSpec((k, bn), lambda i, j: (0, j)),
                  pl.BlockSpec((bm, bn), lambda i, j: (i, j))],
        out_specs=pl.BlockSpec((bm, bn), lambda i, j: (i, j)),
        compiler_params=_params(("parallel", "arbitrary"), 56),
        name="matmul_residual",
    )(a, b, res)


def _group_ones(group):
    r = lax.broadcasted_iota(jnp.int32, (2 * LANES, LANES), 0)
    c = lax.broadcasted_iota(jnp.int32, (2 * LANES, LANES), 1)
    return jnp.where((r % LANES) // group == c // group, 1.0, 0.0).astype(BF16)


def _group_sum(x, ones):
    hi = x.astype(BF16)
    lo = (x - hi.astype(F32)).astype(BF16)
    outs = []
    for c in range(x.shape[1] // LANES):
        sl = slice(LANES * c, LANES * (c + 1))
        outs.append(_dot(jnp.concatenate([hi[:, sl], lo[:, sl]], axis=1), ones))
    return outs[0] if len(outs) == 1 else jnp.concatenate(outs, axis=1)


def _tile_lanes(x, reps):
    return x if reps == 1 else jnp.concatenate([x] * reps, axis=1)


def _shift_rows(u, prev_row, next_row):
    n = u.shape[0]
    row = lax.broadcasted_iota(jnp.int32, u.shape, 0)
    up = jnp.where(row == 0, prev_row, pltpu.roll(u, 1, 0))
    dn = jnp.where(row == n - 1, next_row, pltpu.roll(u, n - 1, 0))
    return up, dn


def _norm_rope(x, g, cos, sin, scale):
    reps = x.shape[1] // LANES
    ss = _group_sum(x * x, _group_ones(A_HEAD_DIM))
    xn = x * lax.rsqrt(ss * (1.0 / A_HEAD_DIM) + EPS) * _tile_lanes(g, reps)
    lane = lax.broadcasted_iota(jnp.int32, x.shape, 1)
    half = A_HEAD_DIM // 2
    rot = jnp.where(lane % A_HEAD_DIM < half,
                    -pltpu.roll(xn, x.shape[1] - half, 1), pltpu.roll(xn, half, 1))
    return (xn * _tile_lanes(cos, reps) + rot * _tile_lanes(sin, reps)) * scale


def _mm_qkprep_kernel(a_ref, b_ref, xq_ref, xk_ref, cos_ref, sin_ref, gq_ref, gk_ref, z_ref, qk_ref):
    z_ref[...] = _dot(a_ref[...], b_ref[...]).astype(z_ref.dtype)
    cos, sin = cos_ref[...], sin_ref[...]
    q = _norm_rope(xq_ref[...].astype(F32), gq_ref[...], cos, sin, A_HEAD_DIM ** -0.5 * LOG2E)
    k = _norm_rope(xk_ref[...].astype(F32), gk_ref[...], cos, sin, 1.0)
    qk_ref[:, :A_WIDTH] = q.astype(qk_ref.dtype)
    qk_ref[:, A_WIDTH:] = k.astype(qk_ref.dtype)


def matmul_with_qk_prep(h, w_r, zqkv, t, q_norm, k_norm, bm=1024, nsub=4):
    m, k = h.shape
    n = w_r.shape[1]
    bm = min(bm, m)
    bn, sub = n // nsub, bm // nsub
    nt = t // sub
    cos, sin = _rope_tables(t)
    gq = jnp.tile(q_norm.reshape(1, A_HEAD_DIM), (1, LANES // A_HEAD_DIM))
    gk = jnp.tile(k_norm.reshape(1, A_HEAD_DIM), (1, LANES // A_HEAD_DIM))
    slab = lambda i, j: i * nsub + j
    return pl.pallas_call(
        _mm_qkprep_kernel,
        out_shape=(jax.ShapeDtypeStruct((m, n), BF16), jax.ShapeDtypeStruct((m, 2 * A_WIDTH), BF16)),
        grid=(m // bm, nsub),
        in_specs=[pl.BlockSpec((bm, k), lambda i, j: (i, 0)),
                  pl.BlockSpec((k, bn), lambda i, j: (0, j)),
                  pl.BlockSpec((sub, A_WIDTH), lambda i, j: (slab(i, j), 0)),
                  pl.BlockSpec((sub, A_WIDTH), lambda i, j: (slab(i, j), 1)),
                  pl.BlockSpec((sub, LANES), lambda i, j: (slab(i, j) % nt, 0)),
                  pl.BlockSpec((sub, LANES), lambda i, j: (slab(i, j) % nt, 0)),
                  pl.BlockSpec((1, LANES), lambda i, j: (0, 0)),
                  pl.BlockSpec((1, LANES), lambda i, j: (0, 0))],
        out_specs=(pl.BlockSpec((bm, bn), lambda i, j: (i, j)),
                   pl.BlockSpec((sub, 2 * A_WIDTH), lambda i, j: (slab(i, j), 0))),
        compiler_params=_params(("parallel", "arbitrary"), 56),
        name="matmul_with_qk_prep",
    )(h, w_r, zqkv, zqkv, cos, sin, gq, gk)


def _dattn_kernel(lq1_ref, lk1_ref, lq2_ref, lk2_ref, g_ref, q_ref, k_ref, v_ref, o_ref,
                  m_ref, acc_ref, s_ref, *, bk, group, lam_init):
    bq = q_ref.shape[1]
    nk = k_ref.shape[1] // bk
    q = q_ref[0]
    lane = lax.broadcasted_iota(jnp.int32, q.shape, 1)
    zero = jnp.zeros_like(q)
    first = lane < A_HEAD_DIM
    qq = jnp.concatenate([jnp.where(first, q, zero), jnp.where(first, zero, q)], axis=0)
    m_ref[...] = jnp.full(m_ref.shape, -jnp.inf, F32)
    acc_ref[...] = jnp.zeros(acc_ref.shape, F32)
    ones = jnp.ones((bk, LANES), BF16)

    def scores(j):
        start = pl.multiple_of(j * bk, bk)
        return _dot(qq, k_ref[0, pl.ds(start, bk), :], NT)

    def accumulate(j, s):
        start = pl.multiple_of(j * bk, bk)
        vs = v_ref[0, pl.ds(start, bk), :]
        m_prev = m_ref[...]
        m_new = jnp.maximum(m_prev, jnp.max(s, axis=-1, keepdims=True))
        alpha = jnp.exp2(m_prev - m_new)
        p = jnp.exp2(s - _tile_lanes(m_new, bk // LANES))
        pv = _dot(p.astype(BF16), jnp.concatenate([vs, ones], axis=1))
        acc_ref[...] = _tile_lanes(alpha, 2) * acc_ref[...] + pv
        m_ref[...] = m_new

    def run_group(j0, prefetch_last):
        for u in range(group):
            if u < group - 1 or prefetch_last:
                s_ref[(u + 1) % 2] = scores(j0 + u + 1)
            accumulate(j0 + u, s_ref[u % 2])

    s_ref[0] = scores(0)

    def body(i, carry):
        run_group(group * i, True)
        return carry

    lax.fori_loop(0, nk // group - 1, body, 0)
    run_group(nk - group, False)
    lam = (jnp.exp(jnp.sum(lq1_ref[...] * lk1_ref[...], keepdims=True))
           - jnp.exp(jnp.sum(lq2_ref[...] * lk2_ref[...], keepdims=True)) + lam_init)
    o12 = acc_ref[:, :LANES] / acc_ref[:, LANES:]
    o = o12[:bq] - lam * o12[bq:]
    ms = jnp.mean(o * o, axis=-1, keepdims=True)
    y = (o * lax.rsqrt(ms + EPS) * g_ref[...]) * (1.0 - lam_init)
    o_ref[0] = y.astype(o_ref.dtype)


def diff_attention(qk, zqkv, lq1, lk1, lq2, lk2, sub_g, lam_init, bq=512, bk=512):
    b, t, _ = qk.shape
    group = 4
    bq, bk = min(bq, t), min(bk, t // group)
    assert (t // bk) % group == 0
    vec = lambda a: a.reshape(1, -1)
    small = lambda n: pl.BlockSpec((1, n), lambda i, h, j: (0, 0))
    return pl.pallas_call(
        functools.partial(_dattn_kernel, bk=bk, group=group, lam_init=lam_init),
        out_shape=jax.ShapeDtypeStruct((b, t, A_WIDTH), BF16),
        grid=(b, A_HEADS, t // bq),
        in_specs=[small(A_HEAD_DIM), small(A_HEAD_DIM), small(A_HEAD_DIM), small(A_HEAD_DIM),
                  small(LANES),
                  pl.BlockSpec((1, bq, LANES), lambda i, h, j: (i, j, h)),
                  pl.BlockSpec((1, t, LANES), lambda i, h, j: (i, 0, A_HEADS + h)),
                  pl.BlockSpec((1, t, LANES), lambda i, h, j: (i, 0, 2 * A_HEADS + h))],
        out_specs=pl.BlockSpec((1, bq, LANES), lambda i, h, j: (i, j, h)),
        scratch_shapes=[pltpu.VMEM((2 * bq, LANES), F32), pltpu.VMEM((2 * bq, 2 * LANES), F32),
                        pltpu.VMEM((2, 2 * bq, bk), F32)],
        compiler_params=_params(("parallel", "parallel", "arbitrary"), 56),
        name="diff_attention",
    )(vec(lq1), vec(lk1), vec(lq2), vec(lk2), vec(sub_g), qk, qk, zqkv)


def _mem_attn_kernel(q_ref, kv_ref, gq_ref, gk_ref, o_ref):
    outs = []
    for h in range(M_HEADS):
        sl = slice(h * M_HEAD_DIM, (h + 1) * M_HEAD_DIM)
        qh = q_ref[0, :, sl].astype(F32)
        qn = qh * lax.rsqrt(jnp.mean(qh * qh, axis=-1, keepdims=True) + EPS) * gq_ref[...]
        kh = kv_ref[0, :, sl]
        kn = kh * lax.rsqrt(jnp.mean(kh * kh, axis=-1, keepdims=True) + EPS) * gk_ref[...]
        vh = kv_ref[0, :, M_WIDTH + h * M_HEAD_DIM:M_WIDTH + (h + 1) * M_HEAD_DIM]
        s = _dot((qn * M_HEAD_DIM ** -0.5).astype(BF16), kn.astype(BF16), NT)
        p = jnp.exp(s - jnp.max(s, axis=-1, keepdims=True))
        o = _dot(p.astype(BF16), vh.astype(BF16)) / jnp.sum(p, axis=-1, keepdims=True)
        outs.append(o)
    o_ref[0] = jnp.concatenate(outs, axis=1).astype(o_ref.dtype)


def memory_attention(zm, kv, mq_norm, mk_norm, bt=512):
    b, t, _ = zm.shape
    bt = min(bt, t)
    return pl.pallas_call(
        _mem_attn_kernel,
        out_shape=jax.ShapeDtypeStruct((b, t, M_WIDTH), BF16),
        grid=(b, t // bt),
        in_specs=[pl.BlockSpec((1, bt, M_WIDTH), lambda i, j: (i, j, 0)),
                  pl.BlockSpec((1, N_MEM, 2 * M_WIDTH), lambda i, j: (i, 0, 0)),
                  pl.BlockSpec((1, M_HEAD_DIM), lambda i, j: (0, 0)),
                  pl.BlockSpec((1, M_HEAD_DIM), lambda i, j: (0, 0))],
        out_specs=pl.BlockSpec((1, bt, M_WIDTH), lambda i, j: (i, j, 0)),
        compiler_params=_params(("parallel", "arbitrary"), 40),
        name="memory_attention",
    )(zm, kv, mq_norm.reshape(1, -1), mk_norm.reshape(1, -1))


def _mm_rwkvprep_kernel(a_ref, b_ref, z_ref, zp_ref, zn_ref, mup_ref, mun_ref, w0_ref, w2_ref, a0_ref, a2_ref,
                        g2_ref, kk_w_ref, ka_ref, rk_ref,
                        gates_ref, r_ref, v_ref, kk_ref, lw_ref, kd_ref, bb_ref, g_ref, bonus_ref, *, nt):
    n_chunks = 3
    cw = b_ref.shape[1] // n_chunks

    def gate_chunk(c):
        cols = slice(c * cw, (c + 1) * cw)
        gates_ref[:, cols] = jax.nn.sigmoid(_dot(a_ref[...], b_ref[:, cols])).astype(gates_ref.dtype)

    gate_chunk(0)
    j = (pl.program_id(0) * pl.num_programs(1) + pl.program_id(1)) % nt
    z = z_ref[...].astype(F32)
    last = zp_ref.shape[0] - 1
    prev_row = jnp.where(j > 0, zp_ref[last:last + 1, :].astype(F32), 0.0)
    next_row = jnp.where(j < nt - 1, zn_ref[0:1, :].astype(F32), 0.0)
    zp, zn = _shift_rows(z, prev_row, next_row)
    z = z + mup_ref[...] * (zp - z) + mun_ref[...] * (zn - z)
    w = R_WIDTH
    r, k, v = z[:, 0:w], z[:, w:2 * w], z[:, 2 * w:3 * w]
    zw = z[:, 3 * w:3 * w + LANES]
    za = z[:, 3 * w + LANES:3 * w + 2 * LANES]
    zg = z[:, 3 * w + 2 * LANES:]
    r_ref[...] = r.astype(r_ref.dtype)
    v_ref[...] = v.astype(v_ref.dtype)
    wlin = w0_ref[...] + _dot(jnp.tanh(zw).astype(BF16), w2_ref[...])
    alin = a0_ref[...] + _dot(za.astype(BF16), a2_ref[...])
    g = _dot(jax.nn.sigmoid(zg).astype(BF16), g2_ref[...])
    g_ref[...] = g.astype(g_ref.dtype)
    ones = _group_ones(R_HEAD)
    kk = k * kk_w_ref[...]
    kk_sq = _group_sum(kk * kk, ones)
    gate_chunk(1)
    neg = -wlin
    softplus = jnp.maximum(neg, 0.0) + jnp.log(1.0 + jnp.exp(-jnp.abs(neg)))
    lw = -jnp.exp(-softplus - 0.5)
    lw_ref[0] = lw[:, :w]
    lw_ref[1] = lw[:, w:]
    a = jax.nn.sigmoid(alin)
    kk = kk / jnp.maximum(jnp.sqrt(kk_sq), 1e-12)
    kk_ref[...] = kk.astype(kk_ref.dtype)
    bb_ref[0] = (kk * a[:, :w]).astype(bb_ref.dtype)
    bb_ref[1] = (kk * a[:, w:]).astype(bb_ref.dtype)
    gate_chunk(2)
    ka = ka_ref[...]
    kd0 = k * (1.0 + (a[:, :w] - 1.0) * ka)
    kd1 = k * (1.0 + (a[:, w:] - 1.0) * ka)
    kd_ref[0] = kd0.astype(kd_ref.dtype)
    kd_ref[1] = kd1.astype(kd_ref.dtype)
    bonus_sum = _group_sum(r * (kd0 + kd1) * rk_ref[...], ones)
    bonus_ref[...] = (bonus_sum * v).astype(bonus_ref.dtype)


def gates_with_rwkv_prep(h, w_g, zr, t, mu_prev, mu_next, decay_w0, decay_w2, aaa_a0, aaa_a2, gate_g2,
                         k_k, k_a, r_k, bm=1024, nsub=4):
    m, k = h.shape
    n = w_g.shape[1]
    cols = zr.shape[1]
    bm = min(bm, m)
    bn, sub = n // nsub, bm // nsub
    w = R_WIDTH
    pad = cols - R_COLS
    mup = jnp.pad(mu_prev, (0, pad)).reshape(1, cols)
    mun = jnp.pad(mu_next, (0, pad)).reshape(1, cols)

    def blockdiag(m):
        z = jnp.zeros_like(m[0])
        return jnp.concatenate([jnp.concatenate([m[0], z], axis=1),
                                jnp.concatenate([z, m[1]], axis=1)], axis=0).astype(BF16)

    g2 = jnp.pad(gate_g2, ((0, GATE_RANK_PAD - GATE_RANK), (0, 0))).astype(BF16)
    row = lambda n: pl.BlockSpec((1, n), lambda i, j: (0, 0))
    full = lambda r_, c_: pl.BlockSpec((r_, c_), lambda i, j: (0, 0))
    halo = 16
    hb = sub // halo
    nh = m // halo
    slab = lambda i, j: i * nsub + j
    tok = pl.BlockSpec((sub, w), lambda i, j: (slab(i, j), 0))
    tok2 = pl.BlockSpec((2, sub, w), lambda i, j: (0, slab(i, j), 0))
    s1 = jax.ShapeDtypeStruct((m, w), BF16)
    s2 = jax.ShapeDtypeStruct((2, m, w), BF16)
    lw_shape = jax.ShapeDtypeStruct((2, m, w), F32)
    return pl.pallas_call(
        functools.partial(_mm_rwkvprep_kernel, nt=t // sub),
        out_shape=(jax.ShapeDtypeStruct((m, n), BF16), s1, s1, s1, lw_shape, s2, s2, s1, s1),
        grid=(m // bm, nsub),
        in_specs=[pl.BlockSpec((bm, k), lambda i, j: (i, 0)),
                  pl.BlockSpec((k, bn), lambda i, j: (0, j)),
                  pl.BlockSpec((sub, cols), lambda i, j: (slab(i, j), 0)),
                  pl.BlockSpec((halo, cols), lambda i, j: (jnp.maximum(slab(i, j) * hb - 1, 0), 0)),
                  pl.BlockSpec((halo, cols), lambda i, j: (jnp.minimum((slab(i, j) + 1) * hb, nh - 1), 0)),
                  row(cols), row(cols), row(2 * w), full(LANES, 2 * w), row(2 * w), full(LANES, 2 * w),
                  full(GATE_RANK_PAD, w), row(w), row(w), row(w)],
        out_specs=(pl.BlockSpec((bm, bn), lambda i, j: (i, j)), tok, tok, tok, tok2, tok2, tok2, tok, tok),
        compiler_params=_params(("parallel", "arbitrary"), 58),
        name="gates_with_rwkv_prep",
    )(h, w_g, zr, zr, zr, mup, mun, decay_w0.reshape(1, 2 * w), blockdiag(decay_w2), aaa_a0.reshape(1, 2 * w),
      blockdiag(aaa_a2), g2, k_k.reshape(1, w), k_a.reshape(1, w), r_k.reshape(1, w))


def _split3(x):
    hi = x.astype(BF16)
    r1 = x - hi.astype(F32)
    mid = r1.astype(BF16)
    lo = (r1 - mid.astype(F32)).astype(BF16)
    return hi, mid, lo


def _rwkv_scan_kernel(rf_ref, vf_ref, kkf_ref, lwf_ref, kdf_ref, bbf_ref,
                      rb_ref, vb_ref, kkb_ref, lwb_ref, kdb_ref, bbb_ref, yf_ref, yb_ref, s_ref):
    C = CHUNK
    n_pairs = R_WIDTH // LANES
    assert C == R_HEAD

    @pl.when(pl.program_id(1) == 0)
    def _():
        s_ref[...] = jnp.zeros(s_ref.shape, F32)

    ti = lax.broadcasted_iota(jnp.int32, (C, C), 0)
    si = lax.broadcasted_iota(jnp.int32, (C, C), 1)
    r2 = lax.broadcasted_iota(jnp.int32, (2 * C, 2 * C), 0)
    c2 = lax.broadcasted_iota(jnp.int32, (2 * C, 2 * C), 1)
    same = (r2 // C) == (c2 // C)
    lane = lax.broadcasted_iota(jnp.int32, (C, LANES), 1)
    row = lax.broadcasted_iota(jnp.int32, (C, LANES), 0)
    first = lane < R_HEAD
    src = lane % C
    eye_h = jnp.where(row == src, 1.0, 0.0)

    def stack(x):
        return jnp.concatenate([jnp.where(first, x, 0.0), jnp.where(first, 0.0, x)], axis=0)

    chains = []
    for d, (r_ref, v_ref, kk_ref, lw_ref, kd_ref, bb_ref, y_ref) in enumerate(
            ((rf_ref, vf_ref, kkf_ref, lwf_ref, kdf_ref, bbf_ref, yf_ref),
             (rb_ref, vb_ref, kkb_ref, lwb_ref, kdb_ref, bbb_ref, yb_ref))):
        sign = 1 if d == 0 else -1
        delta = sign * (r2 - c2)
        strict = same & (delta > 0)
        incl = same & (delta >= 0)
        strict_h = sign * (row - src) > 0
        tri = jnp.where(sign * (ti - si) >= 0, 1.0, 0.0).astype(BF16)
        lw_all = lw_ref[0]
        hi, mid, lo = _split3(lw_all)
        cum_all = _dot(tri, hi) + (_dot(tri, mid) + _dot(tri, lo))
        tot_all = jnp.sum(lw_all, axis=0, keepdims=True)
        for p in range(n_pairs):
            sl = slice(p * LANES, (p + 1) * LANES)
            chains.append(dict(d=d, p=p, sl=sl, strict=strict, incl=incl, strict_h=strict_h,
                               lw=lw_all[:, sl], cum=cum_all[:, sl], tot=tot_all[:, sl],
                               r=r_ref, v=v_ref, kk=kk_ref, kd=kd_ref, bb=bb_ref, y=y_ref))

    for ch in chains:
        sl, lw, cum, tot = ch["sl"], ch["lw"], ch["cum"], ch["tot"]
        r, v, kk = (ch[n][:, sl].astype(F32) for n in ("r", "v", "kk"))
        kd, bb = (ch[n][0, :, sl].astype(F32) for n in ("kd", "bb"))
        e_neg = jnp.exp(-cum)
        e_end = jnp.exp(tot - cum)
        at = -kk * jnp.exp(cum - lw)
        rt = r * jnp.exp(cum)
        ch["ar"] = jnp.concatenate([stack(at), stack(rt)], axis=0).astype(BF16)
        ch["bk"] = jnp.concatenate([stack(bb * e_neg), stack(kd * e_neg)], axis=0).astype(BF16)
        ch["bk_end"] = jnp.concatenate([stack(bb * e_end), stack(kd * e_end)], axis=0).astype(BF16)
        ch["vs"] = stack(v).astype(BF16)
    for ch in chains:
        ch["pm"] = _dot(ch["ar"], ch["bk"], NT)
    for ch in chains:
        pm = ch["pm"]
        ch["lak_mrk"] = jnp.concatenate([jnp.where(ch["strict"], pm[:2 * C, 2 * C:], 0.0),
                                         jnp.where(ch["incl"], pm[2 * C:, 2 * C:], 0.0)],
                                        axis=0).astype(BF16)
        ch["mrb"] = jnp.where(ch["incl"], pm[2 * C:, :2 * C], 0.0).astype(BF16)
        ch["lp"] = jnp.where(ch["strict_h"], jnp.where(first, pm[:C, :2 * C], pm[C:2 * C, :2 * C]), 0.0)
        ch["tm"] = eye_h + ch["lp"]
    n = 1
    while 2 * n < C:
        for ch in chains:
            ch["lp"] = _dot(ch["lp"].astype(BF16), stack(ch["lp"]).astype(BF16))
        for ch in chains:
            ch["tm"] = ch["tm"] + _dot(ch["tm"].astype(BF16), stack(ch["lp"]).astype(BF16))
        n *= 2
    for ch in chains:
        ch["s"] = s_ref[ch["d"], ch["p"]]
        ch["a_s"] = _dot(ch["ar"], ch["s"].astype(BF16), NT)
    for ch in chains:
        ch["lv"] = _dot(ch["lak_mrk"], ch["vs"])
    for ch in chains:
        rhs = (ch["a_s"][:2 * C] + ch["lv"][:2 * C]).astype(BF16)
        ch["ub"] = _dot(stack(ch["tm"]).astype(BF16), rhs).astype(BF16)
    for ch in chains:
        yy = ch["a_s"][2 * C:] + ch["lv"][2 * C:] + _dot(ch["mrb"], ch["ub"])
        ch["y"][:, ch["sl"]] = yy[:C] + yy[C:]
    for ch in chains:
        uv = jnp.concatenate([ch["ub"], ch["vs"]], axis=0)
        s_ref[ch["d"], ch["p"]] = ch["s"] * jnp.exp(ch["tot"]) + _dot(uv, ch["bk_end"], TN)


def rwkv_scan(r, v, kk, lw, kd, bb, t):
    m, w = r.shape
    nc = t // CHUNK
    fwd = pl.BlockSpec((CHUNK, w), lambda i, c: (i * nc + c, 0))
    bwd = pl.BlockSpec((CHUNK, w), lambda i, c: (i * nc + nc - 1 - c, 0))
    fwd2 = pl.BlockSpec((1, CHUNK, w), lambda i, c: (0, i * nc + c, 0))
    bwd2 = pl.BlockSpec((1, CHUNK, w), lambda i, c: (1, i * nc + nc - 1 - c, 0))
    out = jax.ShapeDtypeStruct((m, w), F32)
    return pl.pallas_call(
        _rwkv_scan_kernel,
        out_shape=(out, out),
        grid=(m // t, nc),
        in_specs=[fwd, fwd, fwd, fwd2, fwd2, fwd2, bwd, bwd, bwd, bwd2, bwd2, bwd2],
        out_specs=(fwd, bwd),
        scratch_shapes=[pltpu.VMEM((2, w // LANES, LANES, LANES), F32)],
        compiler_params=_params(("parallel", "arbitrary"), 48),
        name="rwkv_scan",
    )(r, v, kk, lw, kd, bb, r, v, kk, lw, kd, bb)


def _rwkv_post_kernel(yf_ref, yb_ref, bonus_ref, g_ref, w_ref, b_ref, o_ref):
    y = yf_ref[...] + yb_ref[...]
    ones = _group_ones(R_HEAD)
    mu = _group_sum(y, ones) * (1.0 / R_HEAD)
    dlt = y - mu
    var = _group_sum(dlt * dlt, ones) * (1.0 / R_HEAD)
    yn = dlt * lax.rsqrt(var + LNX_EPS) * w_ref[...] + b_ref[...]
    o_ref[...] = ((yn + bonus_ref[...].astype(F32)) * g_ref[...].astype(F32)).astype(o_ref.dtype)


def rwkv_post(y_fwd, y_bwd, bonus, g, lnx_w, lnx_b, bt=512):
    m, w = y_fwd.shape
    bt = min(bt, m)
    tok = pl.BlockSpec((bt, w), lambda i: (i, 0))
    row = pl.BlockSpec((1, w), lambda i: (0, 0))
    return pl.pallas_call(
        _rwkv_post_kernel,
        out_shape=jax.ShapeDtypeStruct((m, w), BF16),
        grid=(m // bt,),
        in_specs=[tok, tok, tok, tok, row, row],
        out_specs=tok,
        compiler_params=_params(("parallel",), 40),
        name="rwkv_post",
    )(y_fwd, y_bwd, bonus, g, lnx_w.reshape(1, w), lnx_b.reshape(1, w))


def _merge_kernel(oa_ref, or_ref, om_ref, pa_ref, pr_ref, pm_ref, ga_ref, gr_ref, gm_ref, o_ref):
    acc = ga_ref[...].astype(F32) * _dot(oa_ref[...], pa_ref[...])
    acc += gr_ref[...].astype(F32) * _dot(or_ref[...], pr_ref[...])
    acc += gm_ref[...].astype(F32) * _dot(om_ref[...], pm_ref[...])
    o_ref[...] = acc.astype(o_ref.dtype)


def gated_merge(o_a, o_r, o_m, p_a, p_r, p_m, gates, bm=1024, bn=1024):
    m, k = o_a.shape
    n = p_a.shape[1]
    bm, bn = min(bm, m), min(bn, n)
    nb = n // bn
    a_spec = pl.BlockSpec((bm, k), lambda i, j: (i, 0))
    w_spec = pl.BlockSpec((k, bn), lambda i, j: (0, j))
    gate = lambda o: pl.BlockSpec((bm, bn), lambda i, j: (i, j + o * nb))
    return pl.pallas_call(
        _merge_kernel,
        out_shape=jax.ShapeDtypeStruct((m, n), BF16),
        grid=(m // bm, nb),
        in_specs=[a_spec, a_spec, a_spec, w_spec, w_spec, w_spec, gate(0), gate(1), gate(2)],
        out_specs=pl.BlockSpec((bm, bn), lambda i, j: (i, j)),
        compiler_params=_params(("parallel", "arbitrary"), 56),
        name="gated_merge",
    )(o_a, o_r, o_m, p_a, p_r, p_m, gates, gates, gates)


def _ffn_up_kernel(h_ref, hp_ref, hn_ref, wg_ref, wv_ref, cg_ref, cv_ref, bg_ref, bv_ref, o_ref, lhs_ref):
    j = pl.program_id(1)
    nt = pl.num_programs(1)
    bm = h_ref.shape[1]
    halo = hp_ref.shape[1]

    @pl.when(pl.program_id(2) == 0)
    def _():
        lhs_ref[0:bm] = h_ref[0]
        lhs_ref[bm:bm + halo] = hp_ref[0]
        lhs_ref[bm + halo:] = hn_ref[0]

    def branch(w_ref, c_ref, b_ref):
        u_all = _dot(lhs_ref[...], w_ref[...])
        u = u_all[:bm]
        prev_row = jnp.where(j > 0, u_all[bm + halo - 1:bm + halo], 0.0)
        next_row = jnp.where(j < nt - 1, u_all[bm + halo:bm + halo + 1], 0.0)
        up, dn = _shift_rows(u, prev_row, next_row)
        c = c_ref[...]
        return c[0:1] * up + c[1:2] * u + c[2:3] * dn + b_ref[...]

    gt = branch(wg_ref, cg_ref, bg_ref)
    val = branch(wv_ref, cv_ref, bv_ref)
    o_ref[0] = (gt * jax.nn.sigmoid(gt) * val).astype(o_ref.dtype)


def ffn_up(h, w_up, conv_w, conv_b, bm=1024, bn=512):
    b, t, k = h.shape
    bm = min(bm, t)
    nb = D_FF // bn
    halo = 16
    hb = bm // halo
    nh = t // halo
    return pl.pallas_call(
        _ffn_up_kernel,
        out_shape=jax.ShapeDtypeStruct((b, t, D_FF), BF16),
        grid=(b, t // bm, nb),
        in_specs=[pl.BlockSpec((1, bm, k), lambda i, j, n: (i, j, 0)),
                  pl.BlockSpec((1, halo, k), lambda i, j, n: (i, jnp.maximum(j * hb - 1, 0), 0)),
                  pl.BlockSpec((1, halo, k), lambda i, j, n: (i, jnp.minimum((j + 1) * hb, nh - 1), 0)),
                  pl.BlockSpec((k, bn), lambda i, j, n: (0, n)),
                  pl.BlockSpec((k, bn), lambda i, j, n: (0, n + nb)),
                  pl.BlockSpec((3, bn), lambda i, j, n: (0, n)),
                  pl.BlockSpec((3, bn), lambda i, j, n: (0, n + nb)),
                  pl.BlockSpec((1, bn), lambda i, j, n: (0, n)),
                  pl.BlockSpec((1, bn), lambda i, j, n: (0, n + nb))],
        out_specs=pl.BlockSpec((1, bm, bn), lambda i, j, n: (i, j, n)),
        scratch_shapes=[pltpu.VMEM((bm + 2 * halo, k), BF16)],
        compiler_params=_params(("parallel", "parallel", "arbitrary"), 56),
        name="ffn_up",
    )(h, h, h, w_up, w_up, conv_w, conv_w, conv_b.reshape(1, -1), conv_b.reshape(1, -1))


def _rope_tables(t):
    half = A_HEAD_DIM // 2
    inv = jnp.power(ROPE_THETA, -jnp.arange(half, dtype=F32) / half)
    ang = jnp.arange(t, dtype=F32)[:, None] * inv[None, :]
    reps = LANES // half
    return jnp.tile(jnp.cos(ang), (1, reps)), jnp.tile(jnp.sin(ang), (1, reps))


def _layer(x, mem, p, w, layer_idx):
    b, t, d = x.shape
    m = b * t
    lam_init = 0.8 - 0.6 * math.exp(-0.3 * layer_idx)
    x2 = x.reshape(m, d)
    h = rmsnorm_rows(x2, p["g_mix"])
    zm = matmul(h, w["w_m"], BF16).reshape(b, t, -1)
    zqkv = matmul(h, w["w_qkv"], BF16)
    zr, qk = matmul_with_qk_prep(h, w["w_r"], zqkv, t, p["q_norm"], p["k_norm"])
    gates, r, v, kk, lw, kd, bb, g, bonus = gates_with_rwkv_prep(
        h, w["w_g"], zr, t, p["mu_prev"], p["mu_next"], p["decay_w0"], p["decay_w2"], p["aaa_a0"],
        p["aaa_a2"], p["gate_g2"], p["k_k"], p["k_a"], p["r_k"])
    y_fwd, y_bwd = rwkv_scan(r, v, kk, lw, kd, bb, t)
    o_r = rwkv_post(y_fwd, y_bwd, bonus, g, p["lnx_w"], p["lnx_b"])
    hm = rmsnorm_rows(mem.reshape(b * N_MEM, d), p["g_memnorm"])
    kv = matmul(hm, w["w_mem_kv"], F32).reshape(b, N_MEM, -1)
    o_m = memory_attention(zm, kv, p["mq_norm"], p["mk_norm"])
    o_a = diff_attention(qk.reshape(b, t, -1), zqkv.reshape(b, t, -1), p["lambda_q1"], p["lambda_k1"],
                         p["lambda_q2"], p["lambda_k2"], p["attn_subln"], lam_init)
    merged = gated_merge(o_a.reshape(m, -1), o_r, o_m.reshape(m, -1),
                         w["p_attn"], w["p_rwkv"], w["p_mem"], gates)
    x2 = matmul_residual(merged, w["w_o"], x2, bn=1024)
    h2 = rmsnorm_rows(x2, p["g_ffn"]).reshape(b, t, d)
    act = ffn_up(h2, w["w_up"], p["conv_w"], p["conv_b"])
    x2 = matmul_residual(act.reshape(m, -1), w["w_down"], x2)
    return x2.reshape(b, t, d)


def _prepare_weights(p):
    w_in = p["w_in"]
    c0 = 3 * A_WIDTH
    c1 = c0 + R_COLS
    c2 = c1 + M_WIDTH
    return {
        "w_qkv": w_in[:, :c0].astype(BF16),
        "w_r": jnp.pad(w_in[:, c0:c1], ((0, 0), (0, R_COLS_PAD - R_COLS))).astype(BF16),
        "w_m": w_in[:, c1:c2].astype(BF16),
        "w_g": w_in[:, c2:].astype(BF16),
        "w_mem_kv": p["w_mem_kv"].astype(BF16),
        "p_attn": p["p_attn"].astype(BF16),
        "p_rwkv": p["p_rwkv"].astype(BF16),
        "p_mem": p["p_mem"].astype(BF16),
        "w_o": p["w_o"].astype(BF16),
        "w_up": p["w_up"].astype(BF16),
        "w_down": p["w_down"].astype(BF16),
    }


def kernel(x_prompt, x_sample, mem_prompt, mem_sample, g_mix, w_in, q_norm, k_norm, lambda_q1, lambda_k1, lambda_q2, lambda_k2, attn_subln, mu_prev, mu_next, decay_w0, decay_w2, aaa_a0, aaa_a2, gate_g2, k_k, k_a, r_k, lnx_w, lnx_b, g_memnorm, w_mem_kv, mq_norm, mk_norm, p_attn, p_rwkv, p_mem, w_o, g_ffn, w_up, conv_w, conv_b, w_down):
    params = dict(g_mix=g_mix, w_in=w_in, q_norm=q_norm, k_norm=k_norm, lambda_q1=lambda_q1,
                  lambda_k1=lambda_k1, lambda_q2=lambda_q2, lambda_k2=lambda_k2, attn_subln=attn_subln,
                  mu_prev=mu_prev, mu_next=mu_next, decay_w0=decay_w0, decay_w2=decay_w2, aaa_a0=aaa_a0,
                  aaa_a2=aaa_a2, gate_g2=gate_g2, k_k=k_k, k_a=k_a, r_k=r_k, lnx_w=lnx_w, lnx_b=lnx_b,
                  g_memnorm=g_memnorm, w_mem_kv=w_mem_kv, mq_norm=mq_norm, mk_norm=mk_norm, p_attn=p_attn,
                  p_rwkv=p_rwkv, p_mem=p_mem, w_o=w_o, g_ffn=g_ffn, w_up=w_up, conv_w=conv_w, conv_b=conv_b,
                  w_down=w_down)
    layers = [{name: arr[l] for name, arr in params.items()} for l in range(g_mix.shape[0])]
    weights = [_prepare_weights(p) for p in layers]
    outs = []
    for x, mem in ((x_prompt, mem_prompt), (x_sample, mem_sample)):
        for l, (p, w) in enumerate(zip(layers, weights)):
            x = _layer(x, mem, p, w, l)
        outs.append(x)
    return tuple(outs)
```

```python
import functools
import math

import jax
import jax.numpy as jnp
from jax import lax
from jax.experimental import pallas as pl
from jax.experimental.pallas import tpu as pltpu

F32 = jnp.float32
BF16 = jnp.bfloat16

D_MODEL = 2048
A_HEADS = 8
A_HEAD_DIM = 64
A_WIDTH = 1024
ROPE_THETA = 10000.0
R_HEAD = 64
R_WIDTH = 1024
R_COLS = 3488
R_COLS_PAD = 3584
GATE_RANK = 160
GATE_RANK_PAD = 256
LNX_EPS = 64e-5
N_MEM = 256
M_HEADS = 4
M_HEAD_DIM = 256
M_WIDTH = 1024
D_FF = 5632
EPS = 1e-6
LANES = 128
CHUNK = 64
LOG2E = 1.4426950408889634

NN = (((1,), (0,)), ((), ()))
NT = (((1,), (1,)), ((), ()))
TN = (((0,), (0,)), ((), ()))


def _params(sem, vmem_mb):
    return pltpu.CompilerParams(dimension_semantics=sem, vmem_limit_bytes=vmem_mb << 20)


def _dot(a, b, dims=NN):
    return lax.dot_general(a, b, dims, preferred_element_type=F32)


def _rmsnorm_kernel(x_ref, g_ref, o_ref):
    x = x_ref[...]
    ms = jnp.mean(x * x, axis=-1, keepdims=True)
    o_ref[...] = (x * lax.rsqrt(ms + EPS) * g_ref[...]).astype(o_ref.dtype)


def rmsnorm_rows(x2, g, bm=512):
    m, d = x2.shape
    bm = min(bm, m)
    return pl.pallas_call(
        _rmsnorm_kernel,
        out_shape=jax.ShapeDtypeStruct((m, d), BF16),
        grid=(m // bm,),
        in_specs=[pl.BlockSpec((bm, d), lambda i: (i, 0)),
                  pl.BlockSpec((1, d), lambda i: (0, 0))],
        out_specs=pl.BlockSpec((bm, d), lambda i: (i, 0)),
        compiler_params=_params(("parallel",), 40),
        name="rmsnorm_rows",
    )(x2, g.reshape(1, d))


def _mm_kernel(a_ref, b_ref, o_ref, *, act):
    acc = _dot(a_ref[...], b_ref[...])
    if act == "sigmoid":
        acc = jax.nn.sigmoid(acc)
    o_ref[...] = acc.astype(o_ref.dtype)


def matmul(a, b, out_dtype, act=None, bm=1024, bn=1024):
    m, k = a.shape
    n = b.shape[1]
    bm, bn = min(bm, m), min(bn, n)
    return pl.pallas_call(
        functools.partial(_mm_kernel, act=act),
        out_shape=jax.ShapeDtypeStruct((m, n), out_dtype),
        grid=(m // bm, n // bn),
        in_specs=[pl.BlockSpec((bm, k), lambda i, j: (i, 0)),
                  pl.BlockSpec((k, bn), lambda i, j: (0, j))],
        out_specs=pl.BlockSpec((bm, bn), lambda i, j: (i, j)),
        compiler_params=_params(("parallel", "arbitrary"), 48),
        name="matmul",
    )(a, b)


def _mm_res_kernel(a_ref, b_ref, r_ref, o_ref):
    o_ref[...] = r_ref[...] + _dot(a_ref[...], b_ref[...])


def matmul_residual(a, b, res, bm=1024, bn=512):
    m, k = a.shape
    n = b.shape[1]
    bm, bn = min(bm, m), min(bn, n)
    return pl.pallas_call(
        _mm_res_kernel,
        out_shape=jax.ShapeDtypeStruct((m, n), F32),
        grid=(m // bm, n // bn),
        in_specs=[pl.BlockSpec((bm, k), lambda i, j: (i, 0)),
                  pl.BlockSpec((k, bn), lambda i, j: (0, j)),
                  pl.BlockSpec((bm, bn), lambda i, j: (i, j))],
        out_specs=pl.BlockSpec((bm, bn), lambda i, j: (i, j)),
        compiler_params=_params(("parallel", "arbitrary"), 56),
        name="matmul_residual",
    )(a, b, res)


def _mm_res_norm_kernel(a_ref, b_ref, r_ref, g_ref, x_ref, h_ref):
    x = r_ref[...] + _dot(a_ref[...], b_ref[...])
    x_ref[...] = x
    ms = jnp.mean(x * x, axis=-1, keepdims=True)
    h_ref[...] = (x * lax.rsqrt(ms + EPS) * g_ref[...]).astype(h_ref.dtype)


def matmul_residual_norm(a, b, res, g, bm=512):
    m, k = a.shape
    n = b.shape[1]
    bm = min(bm, m)
    return pl.pallas_call(
        _mm_res_norm_kernel,
        out_shape=(jax.ShapeDtypeStruct((m, n), F32), jax.ShapeDtypeStruct((m, n), BF16)),
        grid=(m // bm,),
        in_specs=[pl.BlockSpec((bm, k), lambda i: (i, 0)),
                  pl.BlockSpec((k, n), lambda i: (0, 0)),
                  pl.BlockSpec((bm, n), lambda i: (i, 0)),
                  pl.BlockSpec((1, n), lambda i: (0, 0))],
        out_specs=(pl.BlockSpec((bm, n), lambda i: (i, 0)), pl.BlockSpec((bm, n), lambda i: (i, 0))),
        compiler_params=_params(("parallel",), 56),
        name="matmul_residual_norm",
    )(a, b, res, g.reshape(1, n))


def _group_ones(group):
    r = lax.broadcasted_iota(jnp.int32, (2 * LANES, LANES), 0)
    c = lax.broadcasted_iota(jnp.int32, (2 * LANES, LANES), 1)
    return jnp.where((r % LANES) // group == c // group, 1.0, 0.0).astype(BF16)


def _group_sum(x, ones):
    hi = x.astype(BF16)
    lo = (x - hi.astype(F32)).astype(BF16)
    outs = []
    for c in range(x.shape[1] // LANES):
        sl = slice(LANES * c, LANES * (c + 1))
        outs.append(_dot(jnp.concatenate([hi[:, sl], lo[:, sl]], axis=1), ones))
    return outs[0] if len(outs) == 1 else jnp.concatenate(outs, axis=1)


def _tile_lanes(x, reps):
    return x if reps == 1 else jnp.concatenate([x] * reps, axis=1)


def _shift_rows(u, prev_row, next_row):
    n = u.shape[0]
    row = lax.broadcasted_iota(jnp.int32, u.shape, 0)
    up = jnp.where(row == 0, prev_row, pltpu.roll(u, 1, 0))
    dn = jnp.where(row == n - 1, next_row, pltpu.roll(u, n - 1, 0))
    return up, dn


def _norm_rope(x, g, cos, sin, scale):
    reps = x.shape[1] // LANES
    ss = _group_sum(x * x, _group_ones(A_HEAD_DIM))
    xn = x * lax.rsqrt(ss * (1.0 / A_HEAD_DIM) + EPS) * _tile_lanes(g, reps)
    lane = lax.broadcasted_iota(jnp.int32, x.shape, 1)
    half = A_HEAD_DIM // 2
    rot = jnp.where(lane % A_HEAD_DIM < half,
                    -pltpu.roll(xn, x.shape[1] - half, 1), pltpu.roll(xn, half, 1))
    return (xn * _tile_lanes(cos, reps) + rot * _tile_lanes(sin, reps)) * scale


def _mm_qkprep_kernel(a_ref, b_ref, xq_ref, xk_ref, cos_ref, sin_ref, gq_ref, gk_ref, z_ref, qk_ref):
    z_ref[...] = _dot(a_ref[...], b_ref[...]).astype(z_ref.dtype)
    cos, sin = cos_ref[...], sin_ref[...]
    q = _norm_rope(xq_ref[...].astype(F32), gq_ref[...], cos, sin, A_HEAD_DIM ** -0.5 * LOG2E)
    k = _norm_rope(xk_ref[...].astype(F32), gk_ref[...], cos, sin, 1.0)
    qk_ref[:, :A_WIDTH] = q.astype(qk_ref.dtype)
    qk_ref[:, A_WIDTH:] = k.astype(qk_ref.dtype)


def matmul_with_qk_prep(h, w_r, zqkv, t, q_norm, k_norm, bm=1024, nsub=4):
    m, k = h.shape
    n = w_r.shape[1]
    bm = min(bm, m)
    bn, sub = n // nsub, bm // nsub
    nt = t // sub
    cos, sin = _rope_tables(t)
    gq = jnp.tile(q_norm.reshape(1, A_HEAD_DIM), (1, LANES // A_HEAD_DIM))
    gk = jnp.tile(k_norm.reshape(1, A_HEAD_DIM), (1, LANES // A_HEAD_DIM))
    slab = lambda i, j: i * nsub + j
    return pl.pallas_call(
        _mm_qkprep_kernel,
        out_shape=(jax.ShapeDtypeStruct((m, n), BF16), jax.ShapeDtypeStruct((m, 2 * A_WIDTH), BF16)),
        grid=(m // bm, nsub),
        in_specs=[pl.BlockSpec((bm, k), lambda i, j: (i, 0)),
                  pl.BlockSpec((k, bn), lambda i, j: (0, j)),
                  pl.BlockSpec((sub, A_WIDTH), lambda i, j: (slab(i, j), 0)),
                  pl.BlockSpec((sub, A_WIDTH), lambda i, j: (slab(i, j), 1)),
                  pl.BlockSpec((sub, LANES), lambda i, j: (slab(i, j) % nt, 0)),
                  pl.BlockSpec((sub, LANES), lambda i, j: (slab(i, j) % nt, 0)),
                  pl.BlockSpec((1, LANES), lambda i, j: (0, 0)),
                  pl.BlockSpec((1, LANES), lambda i, j: (0, 0))],
        out_specs=(pl.BlockSpec((bm, bn), lambda i, j: (i, j)),
                   pl.BlockSpec((sub, 2 * A_WIDTH), lambda i, j: (slab(i, j), 0))),
        compiler_params=_params(("parallel", "arbitrary"), 56),
        name="matmul_with_qk_prep",
    )(h, w_r, zqkv, zqkv, cos, sin, gq, gk)


def _dattn_kernel(lq1_ref, lk1_ref, lq2_ref, lk2_ref, g_ref, q_ref, k_ref, v_ref, o_ref,
                  m_ref, acc_ref, s_ref, qq_ref, *, bq, bk, group, lam_init):
    t = k_ref.shape[1]
    nk, nq = t // bk, t // bq
    ones = jnp.ones((bk, LANES), BF16)
    lane = lax.broadcasted_iota(jnp.int32, (bq, LANES), 1)
    first = lane < A_HEAD_DIM
    lam = (jnp.exp(jnp.sum(lq1_ref[...] * lk1_ref[...], keepdims=True))
           - jnp.exp(jnp.sum(lq2_ref[...] * lk2_ref[...], keepdims=True)) + lam_init)

    def stack_q(qi, slot):
        start = pl.multiple_of(qi * bq, bq)
        q = q_ref[0, pl.ds(start, bq), :]
        zero = jnp.zeros_like(q)
        qq_ref[slot] = jnp.concatenate([jnp.where(first, q, zero), jnp.where(first, zero, q)], axis=0)

    def scores(slot, j):
        start = pl.multiple_of(j * bk, bk)
        return _dot(qq_ref[slot], k_ref[0, pl.ds(start, bk), :], NT)

    def accumulate(j, s):
        start = pl.multiple_of(j * bk, bk)
        vs = v_ref[0, pl.ds(start, bk), :]
        m_prev = m_ref[...]
        m_new = jnp.maximum(m_prev, jnp.max(s, axis=-1, keepdims=True))
        alpha = jnp.exp2(m_prev - m_new)
        p = jnp.exp2(s - _tile_lanes(m_new, bk // LANES))
        pv = _dot(p.astype(BF16), jnp.concatenate([vs, ones], axis=1))
        acc_ref[...] = _tile_lanes(alpha, 2) * acc_ref[...] + pv
        m_ref[...] = m_new

    def run_group(slot, j0, next_scores):
        for u in range(group):
            if u < group - 1:
                s_ref[(u + 1) % 2] = scores(slot, j0 + u + 1)
            else:
                next_scores()
            accumulate(j0 + u, s_ref[u % 2])

    stack_q(0, 0)
    s_ref[0] = scores(0, 0)

    def q_block(qi, carry):
        slot = qi % 2
        m_ref[...] = jnp.full(m_ref.shape, -jnp.inf, F32)
        acc_ref[...] = jnp.zeros(acc_ref.shape, F32)

        def same_q(j0):
            def issue():
                s_ref[0] = scores(slot, j0 + group)
            return issue

        def body(i, c):
            run_group(slot, group * i, same_q(group * i))
            return c

        lax.fori_loop(0, nk // group - 1, body, 0)

        def next_q():
            stack_q(jnp.minimum(qi + 1, nq - 1), 1 - slot)
            s_ref[0] = scores(1 - slot, 0)

        run_group(slot, nk - group, next_q)
        o12 = acc_ref[:, :LANES] / acc_ref[:, LANES:]
        o = o12[:bq] - lam * o12[bq:]
        ms = jnp.mean(o * o, axis=-1, keepdims=True)
        y = (o * lax.rsqrt(ms + EPS) * g_ref[...]) * (1.0 - lam_init)
        o_ref[0, pl.ds(pl.multiple_of(qi * bq, bq), bq), :] = y.astype(o_ref.dtype)
        return carry

    lax.fori_loop(0, nq, q_block, 0)


def diff_attention(qk, zqkv, lq1, lk1, lq2, lk2, sub_g, lam_init, bq=512, bk=512):
    b, t, _ = qk.shape
    group = 8
    bq, bk = min(bq, t), min(bk, t // group)
    assert (t // bk) % group == 0
    vec = lambda a: a.reshape(1, -1)
    small = lambda n: pl.BlockSpec((1, n), lambda i, h: (0, 0))
    seq = lambda off: pl.BlockSpec((1, t, LANES), lambda i, h: (i, 0, off + h))
    return pl.pallas_call(
        functools.partial(_dattn_kernel, bq=bq, bk=bk, group=group, lam_init=lam_init),
        out_shape=jax.ShapeDtypeStruct((b, t, A_WIDTH), BF16),
        grid=(b, A_HEADS),
        in_specs=[small(A_HEAD_DIM), small(A_HEAD_DIM), small(A_HEAD_DIM), small(A_HEAD_DIM),
                  small(LANES), seq(0), seq(A_HEADS), seq(2 * A_HEADS)],
        out_specs=seq(0),
        scratch_shapes=[pltpu.VMEM((2 * bq, LANES), F32), pltpu.VMEM((2 * bq, 2 * LANES), F32),
                        pltpu.VMEM((2, 2 * bq, bk), F32), pltpu.VMEM((2, 2 * bq, LANES), BF16)],
        compiler_params=_params(("parallel", "arbitrary"), 56),
        name="diff_attention",
    )(vec(lq1), vec(lk1), vec(lq2), vec(lk2), vec(sub_g), qk, qk, zqkv)


def _mem_attn_kernel(q_ref, kv_ref, gq_ref, gk_ref, o_ref):
    outs = []
    for h in range(M_HEADS):
        sl = slice(h * M_HEAD_DIM, (h + 1) * M_HEAD_DIM)
        qh = q_ref[0, :, sl].astype(F32)
        qn = qh * lax.rsqrt(jnp.mean(qh * qh, axis=-1, keepdims=True) + EPS) * gq_ref[...]
        kh = kv_ref[0, :, sl]
        kn = kh * lax.rsqrt(jnp.mean(kh * kh, axis=-1, keepdims=True) + EPS) * gk_ref[...]
        vh = kv_ref[0, :, M_WIDTH + h * M_HEAD_DIM:M_WIDTH + (h + 1) * M_HEAD_DIM]
        s = _dot((qn * M_HEAD_DIM ** -0.5).astype(BF16), kn.astype(BF16), NT)
        p = jnp.exp(s - jnp.max(s, axis=-1, keepdims=True))
        o = _dot(p.astype(BF16), vh.astype(BF16)) / jnp.sum(p, axis=-1, keepdims=True)
        outs.append(o)
    o_ref[0] = jnp.concatenate(outs, axis=1).astype(o_ref.dtype)


def memory_attention(zm, kv, mq_norm, mk_norm, bt=512):
    b, t, _ = zm.shape
    bt = min(bt, t)
    return pl.pallas_call(
        _mem_attn_kernel,
        out_shape=jax.ShapeDtypeStruct((b, t, M_WIDTH), BF16),
        grid=(b, t // bt),
        in_specs=[pl.BlockSpec((1, bt, M_WIDTH), lambda i, j: (i, j, 0)),
                  pl.BlockSpec((1, N_MEM, 2 * M_WIDTH), lambda i, j: (i, 0, 0)),
                  pl.BlockSpec((1, M_HEAD_DIM), lambda i, j: (0, 0)),
                  pl.BlockSpec((1, M_HEAD_DIM), lambda i, j: (0, 0))],
        out_specs=pl.BlockSpec((1, bt, M_WIDTH), lambda i, j: (i, j, 0)),
        compiler_params=_params(("parallel", "arbitrary"), 40),
        name="memory_attention",
    )(zm, kv, mq_norm.reshape(1, -1), mk_norm.reshape(1, -1))


def _mm_rwkvprep_kernel(a_ref, b_ref, z_ref, zp_ref, zn_ref, mup_ref, mun_ref, w0_ref, w2_ref, a0_ref, a2_ref,
                        g2_ref, kk_w_ref, ka_ref, rk_ref,
                        gates_ref, r_ref, v_ref, kk_ref, lw_ref, kd_ref, bb_ref, g_ref, bonus_ref, *, nt):
    n_chunks = 3
    cw = b_ref.shape[1] // n_chunks

    def gate_chunk(c):
        cols = slice(c * cw, (c + 1) * cw)
        gates_ref[:, cols] = jax.nn.sigmoid(_dot(a_ref[...], b_ref[:, cols])).astype(gates_ref.dtype)

    gate_chunk(0)
    j = (pl.program_id(0) * pl.num_programs(1) + pl.program_id(1)) % nt
    z = z_ref[...].astype(F32)
    last = zp_ref.shape[0] - 1
    prev_row = jnp.where(j > 0, zp_ref[last:last + 1, :].astype(F32), 0.0)
    next_row = jnp.where(j < nt - 1, zn_ref[0:1, :].astype(F32), 0.0)
    zp, zn = _shift_rows(z, prev_row, next_row)
    z = z + mup_ref[...] * (zp - z) + mun_ref[...] * (zn - z)
    w = R_WIDTH
    r, k, v = z[:, 0:w], z[:, w:2 * w], z[:, 2 * w:3 * w]
    zw = z[:, 3 * w:3 * w + LANES]
    za = z[:, 3 * w + LANES:3 * w + 2 * LANES]
    zg = z[:, 3 * w + 2 * LANES:]
    r_ref[...] = r.astype(r_ref.dtype)
    v_ref[...] = v.astype(v_ref.dtype)
    wlin = w0_ref[...] + _dot(jnp.tanh(zw).astype(BF16), w2_ref[...])
    alin = a0_ref[...] + _dot(za.astype(BF16), a2_ref[...])
    g = _dot(jax.nn.sigmoid(zg).astype(BF16), g2_ref[...])
    g_ref[...] = g.astype(g_ref.dtype)
    ones = _group_ones(R_HEAD)
    kk = k * kk_w_ref[...]
    kk_sq = _group_sum(kk * kk, ones)
    gate_chunk(1)
    neg = -wlin
    softplus = jnp.maximum(neg, 0.0) + jnp.log(1.0 + jnp.exp(-jnp.abs(neg)))
    lw = -jnp.exp(-softplus - 0.5)
    lw_ref[0] = lw[:, :w]
    lw_ref[1] = lw[:, w:]
    a = jax.nn.sigmoid(alin)
    kk = kk / jnp.maximum(jnp.sqrt(kk_sq), 1e-12)
    kk_ref[...] = kk.astype(kk_ref.dtype)
    bb_ref[0] = (kk * a[:, :w]).astype(bb_ref.dtype)
    bb_ref[1] = (kk * a[:, w:]).astype(bb_ref.dtype)
    gate_chunk(2)
    ka = ka_ref[...]
    kd0 = k * (1.0 + (a[:, :w] - 1.0) * ka)
    kd1 = k * (1.0 + (a[:, w:] - 1.0) * ka)
    kd_ref[0] = kd0.astype(kd_ref.dtype)
    kd_ref[1] = kd1.astype(kd_ref.dtype)
    bonus_sum = _group_sum(r * (kd0 + kd1) * rk_ref[...], ones)
    bonus_ref[...] = (bonus_sum * v).astype(bonus_ref.dtype)


def gates_with_rwkv_prep(h, w_g, zr, t, mu_prev, mu_next, decay_w0, decay_w2, aaa_a0, aaa_a2, gate_g2,
                         k_k, k_a, r_k, bm=1024, nsub=4):
    m, k = h.shape
    n = w_g.shape[1]
    cols = zr.shape[1]
    bm = min(bm, m)
    bn, sub = n // nsub, bm // nsub
    w = R_WIDTH
    pad = cols - R_COLS
    mup = jnp.pad(mu_prev, (0, pad)).reshape(1, cols)
    mun = jnp.pad(mu_next, (0, pad)).reshape(1, cols)

    def blockdiag(m):
        z = jnp.zeros_like(m[0])
        return jnp.concatenate([jnp.concatenate([m[0], z], axis=1),
                                jnp.concatenate([z, m[1]], axis=1)], axis=0).astype(BF16)

    g2 = jnp.pad(gate_g2, ((0, GATE_RANK_PAD - GATE_RANK), (0, 0))).astype(BF16)
    row = lambda n: pl.BlockSpec((1, n), lambda i, j: (0, 0))
    full = lambda r_, c_: pl.BlockSpec((r_, c_), lambda i, j: (0, 0))
    halo = 16
    hb = sub // halo
    nh = m // halo
    slab = lambda i, j: i * nsub + j
    tok = pl.BlockSpec((sub, w), lambda i, j: (slab(i, j), 0))
    tok2 = pl.BlockSpec((2, sub, w), lambda i, j: (0, slab(i, j), 0))
    s1 = jax.ShapeDtypeStruct((m, w), BF16)
    s2 = jax.ShapeDtypeStruct((2, m, w), BF16)
    lw_shape = jax.ShapeDtypeStruct((2, m, w), F32)
    return pl.pallas_call(
        functools.partial(_mm_rwkvprep_kernel, nt=t // sub),
        out_shape=(jax.ShapeDtypeStruct((m, n), BF16), s1, s1, s1, lw_shape, s2, s2, s1, s1),
        grid=(m // bm, nsub),
        in_specs=[pl.BlockSpec((bm, k), lambda i, j: (i, 0)),
                  pl.BlockSpec((k, bn), lambda i, j: (0, j)),
                  pl.BlockSpec((sub, cols), lambda i, j: (slab(i, j), 0)),
                  pl.BlockSpec((halo, cols), lambda i, j: (jnp.maximum(slab(i, j) * hb - 1, 0), 0)),
                  pl.BlockSpec((halo, cols), lambda i, j: (jnp.minimum((slab(i, j) + 1) * hb, nh - 1), 0)),
                  row(cols), row(cols), row(2 * w), full(LANES, 2 * w), row(2 * w), full(LANES, 2 * w),
                  full(GATE_RANK_PAD, w), row(w), row(w), row(w)],
        out_specs=(pl.BlockSpec((bm, bn), lambda i, j: (i, j)), tok, tok, tok, tok2, tok2, tok2, tok, tok),
        compiler_params=_params(("parallel", "arbitrary"), 58),
        name="gates_with_rwkv_prep",
    )(h, w_g, zr, zr, zr, mup, mun, decay_w0.reshape(1, 2 * w), blockdiag(decay_w2), aaa_a0.reshape(1, 2 * w),
      blockdiag(aaa_a2), g2, k_k.reshape(1, w), k_a.reshape(1, w), r_k.reshape(1, w))


def _split3(x):
    hi = x.astype(BF16)
    r1 = x - hi.astype(F32)
    mid = r1.astype(BF16)
    lo = (r1 - mid.astype(F32)).astype(BF16)
    return hi, mid, lo


def _rwkv_scan_kernel(rf_ref, vf_ref, kkf_ref, lwf_ref, kdf_ref, bbf_ref,
                      rb_ref, vb_ref, kkb_ref, lwb_ref, kdb_ref, bbb_ref, yf_ref, yb_ref, s_ref):
    C = CHUNK
    n_pairs = R_WIDTH // LANES
    assert C == R_HEAD

    @pl.when(pl.program_id(1) == 0)
    def _():
        s_ref[...] = jnp.zeros(s_ref.shape, F32)

    ti = lax.broadcasted_iota(jnp.int32, (C, C), 0)
    si = lax.broadcasted_iota(jnp.int32, (C, C), 1)
    r2 = lax.broadcasted_iota(jnp.int32, (2 * C, 2 * C), 0)
    c2 = lax.broadcasted_iota(jnp.int32, (2 * C, 2 * C), 1)
    same = (r2 // C) == (c2 // C)
    lane = lax.broadcasted_iota(jnp.int32, (C, LANES), 1)
    row = lax.broadcasted_iota(jnp.int32, (C, LANES), 0)
    first = lane < R_HEAD
    src = lane % C
    eye_h = jnp.where(row == src, 1.0, 0.0)

    def stack(x):
        return jnp.concatenate([jnp.where(first, x, 0.0), jnp.where(first, 0.0, x)], axis=0)

    n_sub = rf_ref.shape[0] // C
    chains = []
    for d, (r_ref, v_ref, kk_ref, lw_ref, kd_ref, bb_ref, y_ref) in enumerate(
            ((rf_ref, vf_ref, kkf_ref, lwf_ref, kdf_ref, bbf_ref, yf_ref),
             (rb_ref, vb_ref, kkb_ref, lwb_ref, kdb_ref, bbb_ref, yb_ref))):
        sign = 1 if d == 0 else -1
        delta = sign * (r2 - c2)
        strict = same & (delta > 0)
        incl = same & (delta >= 0)
        strict_h = sign * (row - src) > 0
        tri = jnp.where(sign * (ti - si) >= 0, 1.0, 0.0).astype(BF16)
        for q in range(n_sub):
            k = q if d == 0 else n_sub - 1 - q
            rows = slice(k * C, (k + 1) * C)
            lw_all = lw_ref[0, rows, :]
            hi, mid, lo = _split3(lw_all)
            cum_all = _dot(tri, hi) + (_dot(tri, mid) + _dot(tri, lo))
            tot_all = jnp.sum(lw_all, axis=0, keepdims=True)
            for p in range(n_pairs):
                sl = slice(p * LANES, (p + 1) * LANES)
                chains.append(dict(d=d, q=q, p=p, sl=sl, rows=rows, strict=strict, incl=incl,
                                   strict_h=strict_h, lw=lw_all[:, sl], cum=cum_all[:, sl], tot=tot_all[:, sl],
                                   r=r_ref, v=v_ref, kk=kk_ref, kd=kd_ref, bb=bb_ref, y=y_ref))

    for ch in chains:
        sl, rows, lw, cum, tot = ch["sl"], ch["rows"], ch["lw"], ch["cum"], ch["tot"]
        r, v, kk = (ch[n][rows, sl].astype(F32) for n in ("r", "v", "kk"))
        kd, bb = (ch[n][0, rows, sl].astype(F32) for n in ("kd", "bb"))
        e_neg = jnp.exp(-cum)
        e_end = jnp.exp(tot - cum)
        at = -kk * jnp.exp(cum - lw)
        rt = r * jnp.exp(cum)
        ch["ar"] = jnp.concatenate([stack(at), stack(rt)], axis=0).astype(BF16)
        ch["bk"] = jnp.concatenate([stack(bb * e_neg), stack(kd * e_neg)], axis=0).astype(BF16)
        ch["bk_end"] = jnp.concatenate([stack(bb * e_end), stack(kd * e_end)], axis=0).astype(BF16)
        ch["vs"] = stack(v).astype(BF16)
    for ch in chains:
        ch["pm"] = _dot(ch["ar"], ch["bk"], NT)
    for ch in chains:
        pm = ch["pm"]
        ch["lak_mrk"] = jnp.concatenate([jnp.where(ch["strict"], pm[:2 * C, 2 * C:], 0.0),
                                         jnp.where(ch["incl"], pm[2 * C:, 2 * C:], 0.0)],
                                        axis=0).astype(BF16)
        ch["mrb"] = jnp.where(ch["incl"], pm[2 * C:, :2 * C], 0.0).astype(BF16)
        ch["lp"] = jnp.where(ch["strict_h"], jnp.where(first, pm[:C, :2 * C], pm[C:2 * C, :2 * C]), 0.0)
        ch["tm"] = eye_h + ch["lp"]
    n = 1
    while 2 * n < C:
        for ch in chains:
            ch["lp"] = _dot(ch["lp"].astype(BF16), stack(ch["lp"]).astype(BF16))
        for ch in chains:
            ch["tm"] = ch["tm"] + _dot(ch["tm"].astype(BF16), stack(ch["lp"]).astype(BF16))
        n *= 2
    for ch in chains:
        ch["lv"] = _dot(ch["lak_mrk"], ch["vs"])
    state = {(d, p): s_ref[d, p] for d in range(2) for p in range(n_pairs)}
    for q in range(n_sub):
        now = [ch for ch in chains if ch["q"] == q]
        for ch in now:
            ch["a_s"] = _dot(ch["ar"], state[ch["d"], ch["p"]].astype(BF16), NT)
        for ch in now:
            rhs = (ch["a_s"][:2 * C] + ch["lv"][:2 * C]).astype(BF16)
            ch["ub"] = _dot(stack(ch["tm"]).astype(BF16), rhs).astype(BF16)
        for ch in now:
            yy = ch["a_s"][2 * C:] + ch["lv"][2 * C:] + _dot(ch["mrb"], ch["ub"])
            ch["y"][ch["rows"], ch["sl"]] = yy[:C] + yy[C:]
        for ch in now:
            uv = jnp.concatenate([ch["ub"], ch["vs"]], axis=0)
            state[ch["d"], ch["p"]] = (state[ch["d"], ch["p"]] * jnp.exp(ch["tot"])
                                       + _dot(uv, ch["bk_end"], TN))
    for (d, p), s in state.items():
        s_ref[d, p] = s


def rwkv_scan(r, v, kk, lw, kd, bb, t):
    m, w = r.shape
    rows = 2 * CHUNK
    nc = t // rows
    fwd = pl.BlockSpec((rows, w), lambda i, c: (i * nc + c, 0))
    bwd = pl.BlockSpec((rows, w), lambda i, c: (i * nc + nc - 1 - c, 0))
    fwd2 = pl.BlockSpec((1, rows, w), lambda i, c: (0, i * nc + c, 0))
    bwd2 = pl.BlockSpec((1, rows, w), lambda i, c: (1, i * nc + nc - 1 - c, 0))
    out = jax.ShapeDtypeStruct((m, w), F32)
    return pl.pallas_call(
        _rwkv_scan_kernel,
        out_shape=(out, out),
        grid=(m // t, nc),
        in_specs=[fwd, fwd, fwd, fwd2, fwd2, fwd2, bwd, bwd, bwd, bwd2, bwd2, bwd2],
        out_specs=(fwd, bwd),
        scratch_shapes=[pltpu.VMEM((2, w // LANES, LANES, LANES), F32)],
        compiler_params=_params(("parallel", "arbitrary"), 48),
        name="rwkv_scan",
    )(r, v, kk, lw, kd, bb, r, v, kk, lw, kd, bb)


def _rwkv_post_kernel(yf_ref, yb_ref, bonus_ref, g_ref, w_ref, b_ref, o_ref):
    y = yf_ref[...] + yb_ref[...]
    ones = _group_ones(R_HEAD)
    mu = _group_sum(y, ones) * (1.0 / R_HEAD)
    dlt = y - mu
    var = _group_sum(dlt * dlt, ones) * (1.0 / R_HEAD)
    yn = dlt * lax.rsqrt(var + LNX_EPS) * w_ref[...] + b_ref[...]
    o_ref[...] = ((yn + bonus_ref[...].astype(F32)) * g_ref[...].astype(F32)).astype(o_ref.dtype)


def rwkv_post(y_fwd, y_bwd, bonus, g, lnx_w, lnx_b, bt=512):
    m, w = y_fwd.shape
    bt = min(bt, m)
    tok = pl.BlockSpec((bt, w), lambda i: (i, 0))
    row = pl.BlockSpec((1, w), lambda i: (0, 0))
    return pl.pallas_call(
        _rwkv_post_kernel,
        out_shape=jax.ShapeDtypeStruct((m, w), BF16),
        grid=(m // bt,),
        in_specs=[tok, tok, tok, tok, row, row],
        out_specs=tok,
        compiler_params=_params(("parallel",), 40),
        name="rwkv_post",
    )(y_fwd, y_bwd, bonus, g, lnx_w.reshape(1, w), lnx_b.reshape(1, w))


def _merge_kernel(oa_ref, or_ref, om_ref, pa_ref, pr_ref, pm_ref, ga_ref, gr_ref, gm_ref, o_ref):
    acc = ga_ref[...].astype(F32) * _dot(oa_ref[...], pa_ref[...])
    acc += gr_ref[...].astype(F32) * _dot(or_ref[...], pr_ref[...])
    acc += gm_ref[...].astype(F32) * _dot(om_ref[...], pm_ref[...])
    o_ref[...] = acc.astype(o_ref.dtype)


def gated_merge(o_a, o_r, o_m, p_a, p_r, p_m, gates, bm=1024, bn=1024):
    m, k = o_a.shape
    n = p_a.shape[1]
    bm, bn = min(bm, m), min(bn, n)
    nb = n // bn
    a_spec = pl.BlockSpec((bm, k), lambda i, j: (i, 0))
    w_spec = pl.BlockSpec((k, bn), lambda i, j: (0, j))
    gate = lambda o: pl.BlockSpec((bm, bn), lambda i, j: (i, j + o * nb))
    return pl.pallas_call(
        _merge_kernel,
        out_shape=jax.ShapeDtypeStruct((m, n), BF16),
        grid=(m // bm, nb),
        in_specs=[a_spec, a_spec, a_spec, w_spec, w_spec, w_spec, gate(0), gate(1), gate(2)],
        out_specs=pl.BlockSpec((bm, bn), lambda i, j: (i, j)),
        compiler_params=_params(("parallel", "arbitrary"), 56),
        name="gated_merge",
    )(o_a, o_r, o_m, p_a, p_r, p_m, gates, gates, gates)


def _ffn_up_kernel(h_ref, hp_ref, hn_ref, wg_ref, wv_ref, cg_ref, cv_ref, bg_ref, bv_ref, o_ref, lhs_ref):
    j = pl.program_id(1)
    nt = pl.num_programs(1)
    bm = h_ref.shape[1]
    halo = hp_ref.shape[1]

    @pl.when(pl.program_id(2) == 0)
    def _():
        lhs_ref[0:bm] = h_ref[0]
        lhs_ref[bm:bm + halo] = hp_ref[0]
        lhs_ref[bm + halo:] = hn_ref[0]

    def branch(w_ref, c_ref, b_ref):
        u_all = _dot(lhs_ref[...], w_ref[...])
        u = u_all[:bm]
        prev_row = jnp.where(j > 0, u_all[bm + halo - 1:bm + halo], 0.0)
        next_row = jnp.where(j < nt - 1, u_all[bm + halo:bm + halo + 1], 0.0)
        up, dn = _shift_rows(u, prev_row, next_row)
        c = c_ref[...]
        return c[0:1] * up + c[1:2] * u + c[2:3] * dn + b_ref[...]

    gt = branch(wg_ref, cg_ref, bg_ref)
    val = branch(wv_ref, cv_ref, bv_ref)
    o_ref[0] = (gt * jax.nn.sigmoid(gt) * val).astype(o_ref.dtype)


def ffn_up(h, w_up, conv_w, conv_b, bm=1024, bn=512):
    b, t, k = h.shape
    bm = min(bm, t)
    nb = D_FF // bn
    halo = 16
    hb = bm // halo
    nh = t // halo
    return pl.pallas_call(
        _ffn_up_kernel,
        out_shape=jax.ShapeDtypeStruct((b, t, D_FF), BF16),
        grid=(b, t // bm, nb),
        in_specs=[pl.BlockSpec((1, bm, k), lambda i, j, n: (i, j, 0)),
                  pl.BlockSpec((1, halo, k), lambda i, j, n: (i, jnp.maximum(j * hb - 1, 0), 0)),
                  pl.BlockSpec((1, halo, k), lambda i, j, n: (i, jnp.minimum((j + 1) * hb, nh - 1), 0)),
                  pl.BlockSpec((k, bn), lambda i, j, n: (0, n)),
                  pl.BlockSpec((k, bn), lambda i, j, n: (0, n + nb)),
                  pl.BlockSpec((3, bn), lambda i, j, n: (0, n)),
                  pl.BlockSpec((3, bn), lambda i, j, n: (0, n + nb)),
                  pl.BlockSpec((1, bn), lambda i, j, n: (0, n)),
                  pl.BlockSpec((1, bn), lambda i, j, n: (0, n + nb))],
        out_specs=pl.BlockSpec((1, bm, bn), lambda i, j, n: (i, j, n)),
        scratch_shapes=[pltpu.VMEM((bm + 2 * halo, k), BF16)],
        compiler_params=_params(("parallel", "parallel", "arbitrary"), 56),
        name="ffn_up",
    )(h, h, h, w_up, w_up, conv_w, conv_w, conv_b.reshape(1, -1), conv_b.reshape(1, -1))


def _rope_tables(t):
    half = A_HEAD_DIM // 2
    inv = jnp.power(ROPE_THETA, -jnp.arange(half, dtype=F32) / half)
    ang = jnp.arange(t, dtype=F32)[:, None] * inv[None, :]
    reps = LANES // half
    return jnp.tile(jnp.cos(ang), (1, reps)), jnp.tile(jnp.sin(ang), (1, reps))


def _layer(x, mem, p, w, layer_idx):
    b, t, d = x.shape
    m = b * t
    lam_init = 0.8 - 0.6 * math.exp(-0.3 * layer_idx)
    x2 = x.reshape(m, d)
    h = rmsnorm_rows(x2, p["g_mix"])
    zm = matmul(h, w["w_m"], BF16).reshape(b, t, -1)
    zqkv = matmul(h, w["w_qkv"], BF16)
    zr, qk = matmul_with_qk_prep(h, w["w_r"], zqkv, t, p["q_norm"], p["k_norm"])
    gates, r, v, kk, lw, kd, bb, g, bonus = gates_with_rwkv_prep(
        h, w["w_g"], zr, t, p["mu_prev"], p["mu_next"], p["decay_w0"], p["decay_w2"], p["aaa_a0"],
        p["aaa_a2"], p["gate_g2"], p["k_k"], p["k_a"], p["r_k"])
    y_fwd, y_bwd = rwkv_scan(r, v, kk, lw, kd, bb, t)
    o_r = rwkv_post(y_fwd, y_bwd, bonus, g, p["lnx_w"], p["lnx_b"])
    hm = rmsnorm_rows(mem.reshape(b * N_MEM, d), p["g_memnorm"])
    kv = matmul(hm, w["w_mem_kv"], F32).reshape(b, N_MEM, -1)
    o_m = memory_attention(zm, kv, p["mq_norm"], p["mk_norm"])
    o_a = diff_attention(qk.reshape(b, t, -1), zqkv.reshape(b, t, -1), p["lambda_q1"], p["lambda_k1"],
                         p["lambda_q2"], p["lambda_k2"], p["attn_subln"], lam_init)
    merged = gated_merge(o_a.reshape(m, -1), o_r, o_m.reshape(m, -1),
                         w["p_attn"], w["p_rwkv"], w["p_mem"], gates)
    x2, h2 = matmul_residual_norm(merged, w["w_o"], x2, p["g_ffn"])
    act = ffn_up(h2.reshape(b, t, d), w["w_up"], p["conv_w"], p["conv_b"])
    x2 = matmul_residual(act.reshape(m, -1), w["w_down"], x2)
    return x2.reshape(b, t, d)


def _prepare_weights(p):
    w_in = p["w_in"]
    c0 = 3 * A_WIDTH
    c1 = c0 + R_COLS
    c2 = c1 + M_WIDTH
    return {
        "w_qkv": w_in[:, :c0].astype(BF16),
        "w_r": jnp.pad(w_in[:, c0:c1], ((0, 0), (0, R_COLS_PAD - R_COLS))).astype(BF16),
        "w_m": w_in[:, c1:c2].astype(BF16),
        "w_g": w_in[:, c2:].astype(BF16),
        "w_mem_kv": p["w_mem_kv"].astype(BF16),
        "p_attn": p["p_attn"].astype(BF16),
        "p_rwkv": p["p_rwkv"].astype(BF16),
        "p_mem": p["p_mem"].astype(BF16),
        "w_o": p["w_o"].astype(BF16),
        "w_up": p["w_up"].astype(BF16),
        "w_down": p["w_down"].astype(BF16),
    }


def kernel(x_prompt, x_sample, mem_prompt, mem_sample, g_mix, w_in, q_norm, k_norm, lambda_q1, lambda_k1, lambda_q2, lambda_k2, attn_subln, mu_prev, mu_next, decay_w0, decay_w2, aaa_a0, aaa_a2, gate_g2, k_k, k_a, r_k, lnx_w, lnx_b, g_memnorm, w_mem_kv, mq_norm, mk_norm, p_attn, p_rwkv, p_mem, w_o, g_ffn, w_up, conv_w, conv_b, w_down):
    params = dict(g_mix=g_mix, w_in=w_in, q_norm=q_norm, k_norm=k_norm, lambda_q1=lambda_q1,
                  lambda_k1=lambda_k1, lambda_q2=lambda_q2, lambda_k2=lambda_k2, attn_subln=attn_subln,
                  mu_prev=mu_prev, mu_next=mu_next, decay_w0=decay_w0, decay_w2=decay_w2, aaa_a0=aaa_a0,
                  aaa_a2=aaa_a2, gate_g2=gate_g2, k_k=k_k, k_a=k_a, r_k=r_k, lnx_w=lnx_w, lnx_b=lnx_b,
                  g_memnorm=g_memnorm, w_mem_kv=w_mem_kv, mq_norm=mq_norm, mk_norm=mk_norm, p_attn=p_attn,
                  p_rwkv=p_rwkv, p_mem=p_mem, w_o=w_o, g_ffn=g_ffn, w_up=w_up, conv_w=conv_w, conv_b=conv_b,
                  w_down=w_down)
    layers = [{name: arr[l] for name, arr in params.items()} for l in range(g_mix.shape[0])]
    weights = [_prepare_weights(p) for p in layers]
    outs = []
    for x, mem in ((x_prompt, mem_prompt), (x_sample, mem_sample)):
        for l, (p, w) in enumerate(zip(layers, weights)):
            x = _layer(x, mem, p, w, l)
        outs.append(x)
    return tuple(outs)
```

```python
import functools
import math

import jax
import jax.numpy as jnp
from jax import lax
from jax.experimental import pallas as pl
from jax.experimental.pallas import tpu as pltpu

F32 = jnp.float32
BF16 = jnp.bfloat16

D_MODEL = 2048
A_HEADS = 8
A_HEAD_DIM = 64
A_WIDTH = 1024
ROPE_THETA = 10000.0
R_HEAD = 64
R_WIDTH = 1024
R_COLS = 3488
R_COLS_PAD = 3584
GATE_RANK = 160
GATE_RANK_PAD = 256
LNX_EPS = 64e-5
N_MEM = 256
M_HEADS = 4
M_HEAD_DIM = 256
M_WIDTH = 1024
D_FF = 5632
EPS = 1e-6
LANES = 128
CHUNK = 64
INV_BASE_BLOCK = 4
LOG2E = 1.4426950408889634

NN = (((1,), (0,)), ((), ()))
NT = (((1,), (1,)), ((), ()))
TN = (((0,), (0,)), ((), ()))


def _params(sem, vmem_mb):
    return pltpu.CompilerParams(dimension_semantics=sem, vmem_limit_bytes=vmem_mb << 20)


def _dot(a, b, dims=NN):
    return lax.dot_general(a, b, dims, preferred_element_type=F32)


def _rmsnorm_kernel(x_ref, g_ref, o_ref):
    x = x_ref[...]
    ms = jnp.mean(x * x, axis=-1, keepdims=True)
    o_ref[...] = (x * lax.rsqrt(ms + EPS) * g_ref[...]).astype(o_ref.dtype)


def rmsnorm_rows(x2, g, bm=512):
    m, d = x2.shape
    bm = min(bm, m)
    return pl.pallas_call(
        _rmsnorm_kernel,
        out_shape=jax.ShapeDtypeStruct((m, d), BF16),
        grid=(m // bm,),
        in_specs=[pl.BlockSpec((bm, d), lambda i: (i, 0)),
                  pl.BlockSpec((1, d), lambda i: (0, 0))],
        out_specs=pl.BlockSpec((bm, d), lambda i: (i, 0)),
        compiler_params=_params(("parallel",), 40),
        name="rmsnorm_rows",
    )(x2, g.reshape(1, d))


def _mm_kernel(a_ref, b_ref, o_ref, *, act):
    acc = _dot(a_ref[...], b_ref[...])
    if act == "sigmoid":
        acc = jax.nn.sigmoid(acc)
    o_ref[...] = acc.astype(o_ref.dtype)


def matmul(a, b, out_dtype, act=None, bm=1024, bn=1024):
    m, k = a.shape
    n = b.shape[1]
    bm, bn = min(bm, m), min(bn, n)
    return pl.pallas_call(
        functools.partial(_mm_kernel, act=act),
        out_shape=jax.ShapeDtypeStruct((m, n), out_dtype),
        grid=(m // bm, n // bn),
        in_specs=[pl.BlockSpec((bm, k), lambda i, j: (i, 0)),
                  pl.BlockSpec((k, bn), lambda i, j: (0, j))],
        out_specs=pl.BlockSpec((bm, bn), lambda i, j: (i, j)),
        compiler_params=_params(("parallel", "arbitrary"), 48),
        name="matmul",
    )(a, b)


def _mm_res_kernel(a_ref, b_ref, r_ref, o_ref):
    o_ref[...] = r_ref[...] + _dot(a_ref[...], b_ref[...])


def matmul_residual(a, b, res, bm=1024, bn=512):
    m, k = a.shape
    n = b.shape[1]
    bm, bn = min(bm, m), min(bn, n)
    return pl.pallas_call(
        _mm_res_kernel,
        out_shape=jax.ShapeDtypeStruct((m, n), F32),
        grid=(m // bm, n // bn),
        in_specs=[pl.BlockSpec((bm, k), lambda i, j: (i, 0)),
                  pl.BlockSpec((k, bn), lambda i, j: (0, j)),
                  pl.BlockSpec((bm, bn), lambda i, j: (i, j))],
        out_specs=pl.BlockSpec((bm, bn), lambda i, j: (i, j)),
        compiler_params=_params(("parallel", "arbitrary"), 56),
        name="matmul_residual",
    )(a, b, res)


def _mm_res_norm_kernel(a_ref, b_ref, r_ref, g_ref, x_ref, h_ref):
    x = r_ref[...] + _dot(a_ref[...], b_ref[...])
    x_ref[...] = x
    ms = jnp.mean(x * x, axis=-1, keepdims=True)
    h_ref[...] = (x * lax.rsqrt(ms + EPS) * g_ref[...]).astype(h_ref.dtype)


def matmul_residual_norm(a, b, res, g, bm=512):
    m, k = a.shape
    n = b.shape[1]
    bm = min(bm, m)
    return pl.pallas_call(
        _mm_res_norm_kernel,
        out_shape=(jax.ShapeDtypeStruct((m, n), F32), jax.ShapeDtypeStruct((m, n), BF16)),
        grid=(m // bm,),
        in_specs=[pl.BlockSpec((bm, k), lambda i: (i, 0)),
                  pl.BlockSpec((k, n), lambda i: (0, 0)),
                  pl.BlockSpec((bm, n), lambda i: (i, 0)),
                  pl.BlockSpec((1, n), lambda i: (0, 0))],
        out_specs=(pl.BlockSpec((bm, n), lambda i: (i, 0)), pl.BlockSpec((bm, n), lambda i: (i, 0))),
        compiler_params=_params(("parallel",), 56),
        name="matmul_residual_norm",
    )(a, b, res, g.reshape(1, n))


def _group_ones(group):
    r = lax.broadcasted_iota(jnp.int32, (2 * LANES, LANES), 0)
    c = lax.broadcasted_iota(jnp.int32, (2 * LANES, LANES), 1)
    return jnp.where((r % LANES) // group == c // group, 1.0, 0.0).astype(BF16)


def _group_sum(x, ones):
    hi = x.astype(BF16)
    lo = (x - hi.astype(F32)).astype(BF16)
    outs = []
    for c in range(x.shape[1] // LANES):
        sl = slice(LANES * c, LANES * (c + 1))
        outs.append(_dot(jnp.concatenate([hi[:, sl], lo[:, sl]], axis=1), ones))
    return outs[0] if len(outs) == 1 else jnp.concatenate(outs, axis=1)


def _tile_lanes(x, reps):
    return x if reps == 1 else jnp.concatenate([x] * reps, axis=1)


def _shift_rows(u, prev_row, next_row):
    n = u.shape[0]
    row = lax.broadcasted_iota(jnp.int32, u.shape, 0)
    up = jnp.where(row == 0, prev_row, pltpu.roll(u, 1, 0))
    dn = jnp.where(row == n - 1, next_row, pltpu.roll(u, n - 1, 0))
    return up, dn


def _norm_rope(x, g, cos, sin, scale):
    reps = x.shape[1] // LANES
    ss = _group_sum(x * x, _group_ones(A_HEAD_DIM))
    xn = x * lax.rsqrt(ss * (1.0 / A_HEAD_DIM) + EPS) * _tile_lanes(g, reps)
    lane = lax.broadcasted_iota(jnp.int32, x.shape, 1)
    half = A_HEAD_DIM // 2
    rot = jnp.where(lane % A_HEAD_DIM < half,
                    -pltpu.roll(xn, x.shape[1] - half, 1), pltpu.roll(xn, half, 1))
    return (xn * _tile_lanes(cos, reps) + rot * _tile_lanes(sin, reps)) * scale


def _mm_qkprep_kernel(a_ref, b_ref, xq_ref, xk_ref, cos_ref, sin_ref, gq_ref, gk_ref, z_ref, qk_ref):
    z_ref[...] = _dot(a_ref[...], b_ref[...]).astype(z_ref.dtype)
    cos, sin = cos_ref[...], sin_ref[...]
    q = _norm_rope(xq_ref[...].astype(F32), gq_ref[...], cos, sin, A_HEAD_DIM ** -0.5 * LOG2E)
    k = _norm_rope(xk_ref[...].astype(F32), gk_ref[...], cos, sin, 1.0)
    qk_ref[:, :A_WIDTH] = q.astype(qk_ref.dtype)
    qk_ref[:, A_WIDTH:] = k.astype(qk_ref.dtype)


def matmul_with_qk_prep(h, w_r, zqkv, t, q_norm, k_norm, bm=1024, nsub=4):
    m, k = h.shape
    n = w_r.shape[1]
    bm = min(bm, m)
    bn, sub = n // nsub, bm // nsub
    nt = t // sub
    cos, sin = _rope_tables(t)
    gq = jnp.tile(q_norm.reshape(1, A_HEAD_DIM), (1, LANES // A_HEAD_DIM))
    gk = jnp.tile(k_norm.reshape(1, A_HEAD_DIM), (1, LANES // A_HEAD_DIM))
    slab = lambda i, j: i * nsub + j
    return pl.pallas_call(
        _mm_qkprep_kernel,
        out_shape=(jax.ShapeDtypeStruct((m, n), BF16), jax.ShapeDtypeStruct((m, 2 * A_WIDTH), BF16)),
        grid=(m // bm, nsub),
        in_specs=[pl.BlockSpec((bm, k), lambda i, j: (i, 0)),
                  pl.BlockSpec((k, bn), lambda i, j: (0, j)),
                  pl.BlockSpec((sub, A_WIDTH), lambda i, j: (slab(i, j), 0)),
                  pl.BlockSpec((sub, A_WIDTH), lambda i, j: (slab(i, j), 1)),
                  pl.BlockSpec((sub, LANES), lambda i, j: (slab(i, j) % nt, 0)),
                  pl.BlockSpec((sub, LANES), lambda i, j: (slab(i, j) % nt, 0)),
                  pl.BlockSpec((1, LANES), lambda i, j: (0, 0)),
                  pl.BlockSpec((1, LANES), lambda i, j: (0, 0))],
        out_specs=(pl.BlockSpec((bm, bn), lambda i, j: (i, j)),
                   pl.BlockSpec((sub, 2 * A_WIDTH), lambda i, j: (slab(i, j), 0))),
        compiler_params=_params(("parallel", "arbitrary"), 56),
        name="matmul_with_qk_prep",
    )(h, w_r, zqkv, zqkv, cos, sin, gq, gk)


def _dattn_kernel(lq1_ref, lk1_ref, lq2_ref, lk2_ref, g_ref, q_ref, k_ref, v_ref, o_ref,
                  m_ref, acc_ref, s_ref, qq_ref, *, bq, bk, group, lam_init):
    t = k_ref.shape[1]
    nk, nq = t // bk, t // bq
    ones = jnp.ones((bk, LANES), BF16)
    lane = lax.broadcasted_iota(jnp.int32, (bq, LANES), 1)
    first = lane < A_HEAD_DIM
    lam = (jnp.exp(jnp.sum(lq1_ref[...] * lk1_ref[...], keepdims=True))
           - jnp.exp(jnp.sum(lq2_ref[...] * lk2_ref[...], keepdims=True)) + lam_init)

    def stack_q(qi, slot):
        start = pl.multiple_of(qi * bq, bq)
        q = q_ref[0, pl.ds(start, bq), :]
        zero = jnp.zeros_like(q)
        qq_ref[slot] = jnp.concatenate([jnp.where(first, q, zero), jnp.where(first, zero, q)], axis=0)

    def scores(slot, j):
        start = pl.multiple_of(j * bk, bk)
        return _dot(qq_ref[slot], k_ref[0, pl.ds(start, bk), :], NT)

    def accumulate(j, s):
        start = pl.multiple_of(j * bk, bk)
        vs = v_ref[0, pl.ds(start, bk), :]
        m_prev = m_ref[...]
        m_new = jnp.maximum(m_prev, jnp.max(s, axis=-1, keepdims=True))
        alpha = jnp.exp2(m_prev - m_new)
        p = jnp.exp2(s - _tile_lanes(m_new, bk // LANES))
        pv = _dot(p.astype(BF16), jnp.concatenate([vs, ones], axis=1))
        acc_ref[...] = _tile_lanes(alpha, 2) * acc_ref[...] + pv
        m_ref[...] = m_new

    def run_group(slot, j0, next_scores):
        for u in range(group):
            if u < group - 1:
                s_ref[(u + 1) % 2] = scores(slot, j0 + u + 1)
            else:
                next_scores()
            accumulate(j0 + u, s_ref[u % 2])

    stack_q(0, 0)
    s_ref[0] = scores(0, 0)

    def q_block(qi, carry):
        slot = qi % 2
        m_ref[...] = jnp.full(m_ref.shape, -jnp.inf, F32)
        acc_ref[...] = jnp.zeros(acc_ref.shape, F32)

        def same_q(j0):
            def issue():
                s_ref[0] = scores(slot, j0 + group)
            return issue

        def body(i, c):
            run_group(slot, group * i, same_q(group * i))
            return c

        lax.fori_loop(0, nk // group - 1, body, 0)

        def next_q():
            stack_q(jnp.minimum(qi + 1, nq - 1), 1 - slot)
            s_ref[0] = scores(1 - slot, 0)

        run_group(slot, nk - group, next_q)
        o12 = acc_ref[:, :LANES] / acc_ref[:, LANES:]
        o = o12[:bq] - lam * o12[bq:]
        ms = jnp.mean(o * o, axis=-1, keepdims=True)
        y = (o * lax.rsqrt(ms + EPS) * g_ref[...]) * (1.0 - lam_init)
        o_ref[0, pl.ds(pl.multiple_of(qi * bq, bq), bq), :] = y.astype(o_ref.dtype)
        return carry

    lax.fori_loop(0, nq, q_block, 0)


def diff_attention(qk, zqkv, lq1, lk1, lq2, lk2, sub_g, lam_init, bq=512, bk=512):
    b, t, _ = qk.shape
    group = 8
    bq, bk = min(bq, t), min(bk, t // group)
    assert (t // bk) % group == 0
    vec = lambda a: a.reshape(1, -1)
    small = lambda n: pl.BlockSpec((1, n), lambda i, h: (0, 0))
    seq = lambda off: pl.BlockSpec((1, t, LANES), lambda i, h: (i, 0, off + h))
    return pl.pallas_call(
        functools.partial(_dattn_kernel, bq=bq, bk=bk, group=group, lam_init=lam_init),
        out_shape=jax.ShapeDtypeStruct((b, t, A_WIDTH), BF16),
        grid=(b, A_HEADS),
        in_specs=[small(A_HEAD_DIM), small(A_HEAD_DIM), small(A_HEAD_DIM), small(A_HEAD_DIM),
                  small(LANES), seq(0), seq(A_HEADS), seq(2 * A_HEADS)],
        out_specs=seq(0),
        scratch_shapes=[pltpu.VMEM((2 * bq, LANES), F32), pltpu.VMEM((2 * bq, 2 * LANES), F32),
                        pltpu.VMEM((2, 2 * bq, bk), F32), pltpu.VMEM((2, 2 * bq, LANES), BF16)],
        compiler_params=_params(("parallel", "arbitrary"), 56),
        name="diff_attention",
    )(vec(lq1), vec(lk1), vec(lq2), vec(lk2), vec(sub_g), qk, qk, zqkv)


def _mem_attn_kernel(q_ref, kv_ref, gq_ref, gk_ref, o_ref):
    outs = []
    for h in range(M_HEADS):
        sl = slice(h * M_HEAD_DIM, (h + 1) * M_HEAD_DIM)
        qh = q_ref[0, :, sl].astype(F32)
        qn = qh * lax.rsqrt(jnp.mean(qh * qh, axis=-1, keepdims=True) + EPS) * gq_ref[...]
        kh = kv_ref[0, :, sl]
        kn = kh * lax.rsqrt(jnp.mean(kh * kh, axis=-1, keepdims=True) + EPS) * gk_ref[...]
        vh = kv_ref[0, :, M_WIDTH + h * M_HEAD_DIM:M_WIDTH + (h + 1) * M_HEAD_DIM]
        s = _dot((qn * M_HEAD_DIM ** -0.5).astype(BF16), kn.astype(BF16), NT)
        p = jnp.exp(s - jnp.max(s, axis=-1, keepdims=True))
        o = _dot(p.astype(BF16), vh.astype(BF16)) / jnp.sum(p, axis=-1, keepdims=True)
        outs.append(o)
    o_ref[0] = jnp.concatenate(outs, axis=1).astype(o_ref.dtype)


def memory_attention(zm, kv, mq_norm, mk_norm, bt=512):
    b, t, _ = zm.shape
    bt = min(bt, t)
    return pl.pallas_call(
        _mem_attn_kernel,
        out_shape=jax.ShapeDtypeStruct((b, t, M_WIDTH), BF16),
        grid=(b, t // bt),
        in_specs=[pl.BlockSpec((1, bt, M_WIDTH), lambda i, j: (i, j, 0)),
                  pl.BlockSpec((1, N_MEM, 2 * M_WIDTH), lambda i, j: (i, 0, 0)),
                  pl.BlockSpec((1, M_HEAD_DIM), lambda i, j: (0, 0)),
                  pl.BlockSpec((1, M_HEAD_DIM), lambda i, j: (0, 0))],
        out_specs=pl.BlockSpec((1, bt, M_WIDTH), lambda i, j: (i, j, 0)),
        compiler_params=_params(("parallel", "arbitrary"), 40),
        name="memory_attention",
    )(zm, kv, mq_norm.reshape(1, -1), mk_norm.reshape(1, -1))


def _mm_rwkvprep_kernel(a_ref, b_ref, z_ref, zp_ref, zn_ref, mup_ref, mun_ref, w0_ref, w2_ref, a0_ref, a2_ref,
                        g2_ref, kk_w_ref, ka_ref, rk_ref,
                        gates_ref, r_ref, v_ref, kk_ref, lw_ref, kd_ref, bb_ref, g_ref, bonus_ref, *, nt):
    n_chunks = 3
    cw = b_ref.shape[1] // n_chunks

    def gate_chunk(c):
        cols = slice(c * cw, (c + 1) * cw)
        gates_ref[:, cols] = jax.nn.sigmoid(_dot(a_ref[...], b_ref[:, cols])).astype(gates_ref.dtype)

    gate_chunk(0)
    j = (pl.program_id(0) * pl.num_programs(1) + pl.program_id(1)) % nt
    z = z_ref[...].astype(F32)
    last = zp_ref.shape[0] - 1
    prev_row = jnp.where(j > 0, zp_ref[last:last + 1, :].astype(F32), 0.0)
    next_row = jnp.where(j < nt - 1, zn_ref[0:1, :].astype(F32), 0.0)
    zp, zn = _shift_rows(z, prev_row, next_row)
    z = z + mup_ref[...] * (zp - z) + mun_ref[...] * (zn - z)
    w = R_WIDTH
    r, k, v = z[:, 0:w], z[:, w:2 * w], z[:, 2 * w:3 * w]
    zw = z[:, 3 * w:3 * w + LANES]
    za = z[:, 3 * w + LANES:3 * w + 2 * LANES]
    zg = z[:, 3 * w + 2 * LANES:]
    r_ref[...] = r.astype(r_ref.dtype)
    v_ref[...] = v.astype(v_ref.dtype)
    wlin = w0_ref[...] + _dot(jnp.tanh(zw).astype(BF16), w2_ref[...])
    alin = a0_ref[...] + _dot(za.astype(BF16), a2_ref[...])
    g = _dot(jax.nn.sigmoid(zg).astype(BF16), g2_ref[...])
    g_ref[...] = g.astype(g_ref.dtype)
    ones = _group_ones(R_HEAD)
    kk = k * kk_w_ref[...]
    kk_sq = _group_sum(kk * kk, ones)
    gate_chunk(1)
    lw = -math.exp(-0.5) * jax.nn.sigmoid(wlin)
    lw_ref[0] = lw[:, :w]
    lw_ref[1] = lw[:, w:]
    a = jax.nn.sigmoid(alin)
    kk = kk / jnp.maximum(jnp.sqrt(kk_sq), 1e-12)
    kk_ref[...] = kk.astype(kk_ref.dtype)
    bb_ref[0] = (kk * a[:, :w]).astype(bb_ref.dtype)
    bb_ref[1] = (kk * a[:, w:]).astype(bb_ref.dtype)
    gate_chunk(2)
    ka = ka_ref[...]
    kd0 = k * (1.0 + (a[:, :w] - 1.0) * ka)
    kd1 = k * (1.0 + (a[:, w:] - 1.0) * ka)
    kd_ref[0] = kd0.astype(kd_ref.dtype)
    kd_ref[1] = kd1.astype(kd_ref.dtype)
    bonus_sum = _group_sum(r * (kd0 + kd1) * rk_ref[...], ones)
    bonus_ref[...] = (bonus_sum * v).astype(bonus_ref.dtype)


def gates_with_rwkv_prep(h, w_g, zr, t, mu_prev, mu_next, decay_w0, decay_w2, aaa_a0, aaa_a2, gate_g2,
                         k_k, k_a, r_k, bm=1024, nsub=4):
    m, k = h.shape
    n = w_g.shape[1]
    cols = zr.shape[1]
    bm = min(bm, m)
    bn, sub = n // nsub, bm // nsub
    w = R_WIDTH
    pad = cols - R_COLS
    mup = jnp.pad(mu_prev, (0, pad)).reshape(1, cols)
    mun = jnp.pad(mu_next, (0, pad)).reshape(1, cols)

    def blockdiag(m):
        z = jnp.zeros_like(m[0])
        return jnp.concatenate([jnp.concatenate([m[0], z], axis=1),
                                jnp.concatenate([z, m[1]], axis=1)], axis=0).astype(BF16)

    g2 = jnp.pad(gate_g2, ((0, GATE_RANK_PAD - GATE_RANK), (0, 0))).astype(BF16)
    row = lambda n: pl.BlockSpec((1, n), lambda i, j: (0, 0))
    full = lambda r_, c_: pl.BlockSpec((r_, c_), lambda i, j: (0, 0))
    halo = 16
    hb = sub // halo
    nh = m // halo
    slab = lambda i, j: i * nsub + j
    tok = pl.BlockSpec((sub, w), lambda i, j: (slab(i, j), 0))
    tok2 = pl.BlockSpec((2, sub, w), lambda i, j: (0, slab(i, j), 0))
    s1 = jax.ShapeDtypeStruct((m, w), BF16)
    s2 = jax.ShapeDtypeStruct((2, m, w), BF16)
    lw_shape = jax.ShapeDtypeStruct((2, m, w), F32)
    return pl.pallas_call(
        functools.partial(_mm_rwkvprep_kernel, nt=t // sub),
        out_shape=(jax.ShapeDtypeStruct((m, n), BF16), s1, s1, s1, lw_shape, s2, s2, s1, s1),
        grid=(m // bm, nsub),
        in_specs=[pl.BlockSpec((bm, k), lambda i, j: (i, 0)),
                  pl.BlockSpec((k, bn), lambda i, j: (0, j)),
                  pl.BlockSpec((sub, cols), lambda i, j: (slab(i, j), 0)),
                  pl.BlockSpec((halo, cols), lambda i, j: (jnp.maximum(slab(i, j) * hb - 1, 0), 0)),
                  pl.BlockSpec((halo, cols), lambda i, j: (jnp.minimum((slab(i, j) + 1) * hb, nh - 1), 0)),
                  row(cols), row(cols), row(2 * w), full(LANES, 2 * w), row(2 * w), full(LANES, 2 * w),
                  full(GATE_RANK_PAD, w), row(w), row(w), row(w)],
        out_specs=(pl.BlockSpec((bm, bn), lambda i, j: (i, j)), tok, tok, tok, tok2, tok2, tok2, tok, tok),
        compiler_params=_params(("parallel", "arbitrary"), 58),
        name="gates_with_rwkv_prep",
    )(h, w_g, zr, zr, zr, mup, mun, decay_w0.reshape(1, 2 * w), blockdiag(decay_w2), aaa_a0.reshape(1, 2 * w),
      blockdiag(aaa_a2), g2, k_k.reshape(1, w), k_a.reshape(1, w), r_k.reshape(1, w))


def _split3(x):
    hi = x.astype(BF16)
    r1 = x - hi.astype(F32)
    mid = r1.astype(BF16)
    lo = (r1 - mid.astype(F32)).astype(BF16)
    return hi, mid, lo


def _rwkv_scan_kernel(rf_ref, vf_ref, kkf_ref, lwf_ref, kdf_ref, bbf_ref,
                      rb_ref, vb_ref, kkb_ref, lwb_ref, kdb_ref, bbb_ref, yf_ref, yb_ref, s_ref):
    C = CHUNK
    n_pairs = R_WIDTH // LANES
    assert C == R_HEAD

    @pl.when(pl.program_id(1) == 0)
    def _():
        s_ref[...] = jnp.zeros(s_ref.shape, F32)

    ti = lax.broadcasted_iota(jnp.int32, (C, C), 0)
    si = lax.broadcasted_iota(jnp.int32, (C, C), 1)
    r2 = lax.broadcasted_iota(jnp.int32, (2 * C, 2 * C), 0)
    c2 = lax.broadcasted_iota(jnp.int32, (2 * C, 2 * C), 1)
    same = (r2 // C) == (c2 // C)
    lane = lax.broadcasted_iota(jnp.int32, (C, LANES), 1)
    row = lax.broadcasted_iota(jnp.int32, (C, LANES), 0)
    first = lane < R_HEAD
    src = lane % C
    eye_h = jnp.where(row == src, 1.0, 0.0)

    def stack(x):
        return jnp.concatenate([jnp.where(first, x, 0.0), jnp.where(first, 0.0, x)], axis=0)

    n_sub = rf_ref.shape[0] // C
    chains = []
    for d, (r_ref, v_ref, kk_ref, lw_ref, kd_ref, bb_ref, y_ref) in enumerate(
            ((rf_ref, vf_ref, kkf_ref, lwf_ref, kdf_ref, bbf_ref, yf_ref),
             (rb_ref, vb_ref, kkb_ref, lwb_ref, kdb_ref, bbb_ref, yb_ref))):
        sign = 1 if d == 0 else -1
        delta = sign * (r2 - c2)
        strict = same & (delta > 0)
        incl = same & (delta >= 0)
        strict_h = sign * (row - src) > 0
        tri = jnp.where(sign * (ti - si) >= 0, 1.0, 0.0).astype(BF16)
        for q in range(n_sub):
            k = q if d == 0 else n_sub - 1 - q
            rows = slice(k * C, (k + 1) * C)
            lw_all = lw_ref[0, rows, :]
            hi, mid, lo = _split3(lw_all)
            cum_all = _dot(tri, hi) + (_dot(tri, mid) + _dot(tri, lo))
            tot_all = jnp.sum(lw_all, axis=0, keepdims=True)
            for p in range(n_pairs):
                sl = slice(p * LANES, (p + 1) * LANES)
                chains.append(dict(d=d, q=q, p=p, sl=sl, rows=rows, strict=strict, incl=incl,
                                   strict_h=strict_h, lw=lw_all[:, sl], cum=cum_all[:, sl], tot=tot_all[:, sl],
                                   r=r_ref, v=v_ref, kk=kk_ref, kd=kd_ref, bb=bb_ref, y=y_ref))

    for ch in chains:
        sl, rows, lw, cum, tot = ch["sl"], ch["rows"], ch["lw"], ch["cum"], ch["tot"]
        r, v, kk = (ch[n][rows, sl].astype(F32) for n in ("r", "v", "kk"))
        kd, bb = (ch[n][0, rows, sl].astype(F32) for n in ("kd", "bb"))
        e_neg = jnp.exp(-cum)
        e_end = jnp.exp(tot - cum)
        at = -kk * jnp.exp(cum - lw)
        rt = r * jnp.exp(cum)
        ch["ar"] = jnp.concatenate([stack(at), stack(rt)], axis=0).astype(BF16)
        ch["bk"] = jnp.concatenate([stack(bb * e_neg), stack(kd * e_neg)], axis=0).astype(BF16)
        ch["bk_end"] = jnp.concatenate([stack(bb * e_end), stack(kd * e_end)], axis=0).astype(BF16)
        ch["vs"] = stack(v).astype(BF16)
    for ch in chains:
        ch["pm"] = _dot(ch["ar"], ch["bk"], NT)
    for ch in chains:
        pm = ch["pm"]
        ch["lak_mrk"] = jnp.concatenate([jnp.where(ch["strict"], pm[:2 * C, 2 * C:], 0.0),
                                         jnp.where(ch["incl"], pm[2 * C:, 2 * C:], 0.0)],
                                        axis=0).astype(BF16)
        ch["mrb"] = jnp.where(ch["incl"], pm[2 * C:, :2 * C], 0.0).astype(BF16)
        ch["lh"] = jnp.where(ch["strict_h"], jnp.where(first, pm[:C, :2 * C], pm[C:2 * C, :2 * C]), 0.0)
    for ch in chains:
        ch["lv"] = _dot(ch["lak_mrk"], ch["vs"])
    def product(left, right):
        return _dot(left.astype(BF16), stack(right).astype(BF16))

    blk = INV_BASE_BLOCK
    diag = row // blk == src // blk
    for ch in chains:
        ch["dg"] = jnp.where(diag, ch["lh"], 0.0)
    for ch in chains:
        ch["dg2"] = product(ch["dg"], ch["dg"])
    for ch in chains:
        ch["tm"] = (eye_h + ch["dg"]) + product(eye_h + ch["dg"], ch["dg2"])
    while blk < C:
        couple = (row // (2 * blk) == src // (2 * blk)) & (row // blk != src // blk)
        for ch in chains:
            ch["x"] = product(jnp.where(couple, ch["lh"], 0.0), ch["tm"])
        for ch in chains:
            ch["tm"] = ch["tm"] + product(ch["tm"], ch["x"])
        blk *= 2
    state = {(d, p): s_ref[d, p] for d in range(2) for p in range(n_pairs)}
    for q in range(n_sub):
        now = [ch for ch in chains if ch["q"] == q]
        for ch in now:
            ch["a_s"] = _dot(ch["ar"], state[ch["d"], ch["p"]].astype(BF16), NT)
        for ch in now:
            rhs = (ch["a_s"][:2 * C] + ch["lv"][:2 * C]).astype(BF16)
            ch["ub"] = _dot(stack(ch["tm"]).astype(BF16), rhs).astype(BF16)
        for ch in now:
            yy = ch["a_s"][2 * C:] + ch["lv"][2 * C:] + _dot(ch["mrb"], ch["ub"])
            ch["y"][ch["rows"], ch["sl"]] = yy[:C] + yy[C:]
        for ch in now:
            uv = jnp.concatenate([ch["ub"], ch["vs"]], axis=0)
            state[ch["d"], ch["p"]] = (state[ch["d"], ch["p"]] * jnp.exp(ch["tot"])
                                       + _dot(uv, ch["bk_end"], TN))
    for (d, p), s in state.items():
        s_ref[d, p] = s


def rwkv_scan(r, v, kk, lw, kd, bb, t):
    m, w = r.shape
    rows = 2 * CHUNK
    nc = t // rows
    fwd = pl.BlockSpec((rows, w), lambda i, c: (i * nc + c, 0))
    bwd = pl.BlockSpec((rows, w), lambda i, c: (i * nc + nc - 1 - c, 0))
    fwd2 = pl.BlockSpec((1, rows, w), lambda i, c: (0, i * nc + c, 0))
    bwd2 = pl.BlockSpec((1, rows, w), lambda i, c: (1, i * nc + nc - 1 - c, 0))
    out = jax.ShapeDtypeStruct((m, w), F32)
    return pl.pallas_call(
        _rwkv_scan_kernel,
        out_shape=(out, out),
        grid=(m // t, nc),
        in_specs=[fwd, fwd, fwd, fwd2, fwd2, fwd2, bwd, bwd, bwd, bwd2, bwd2, bwd2],
        out_specs=(fwd, bwd),
        scratch_shapes=[pltpu.VMEM((2, w // LANES, LANES, LANES), F32)],
        compiler_params=_params(("parallel", "arbitrary"), 48),
        name="rwkv_scan",
    )(r, v, kk, lw, kd, bb, r, v, kk, lw, kd, bb)


def _rwkv_post_kernel(yf_ref, yb_ref, bonus_ref, g_ref, w_ref, b_ref, o_ref):
    y = yf_ref[...] + yb_ref[...]
    ones = _group_ones(R_HEAD)
    mu = _group_sum(y, ones) * (1.0 / R_HEAD)
    dlt = y - mu
    var = _group_sum(dlt * dlt, ones) * (1.0 / R_HEAD)
    yn = dlt * lax.rsqrt(var + LNX_EPS) * w_ref[...] + b_ref[...]
    o_ref[...] = ((yn + bonus_ref[...].astype(F32)) * g_ref[...].astype(F32)).astype(o_ref.dtype)


def rwkv_post(y_fwd, y_bwd, bonus, g, lnx_w, lnx_b, bt=512):
    m, w = y_fwd.shape
    bt = min(bt, m)
    tok = pl.BlockSpec((bt, w), lambda i: (i, 0))
    row = pl.BlockSpec((1, w), lambda i: (0, 0))
    return pl.pallas_call(
        _rwkv_post_kernel,
        out_shape=jax.ShapeDtypeStruct((m, w), BF16),
        grid=(m // bt,),
        in_specs=[tok, tok, tok, tok, row, row],
        out_specs=tok,
        compiler_params=_params(("parallel",), 40),
        name="rwkv_post",
    )(y_fwd, y_bwd, bonus, g, lnx_w.reshape(1, w), lnx_b.reshape(1, w))


def _merge_kernel(oa_ref, or_ref, om_ref, pa_ref, pr_ref, pm_ref, ga_ref, gr_ref, gm_ref, o_ref):
    acc = ga_ref[...].astype(F32) * _dot(oa_ref[...], pa_ref[...])
    acc += gr_ref[...].astype(F32) * _dot(or_ref[...], pr_ref[...])
    acc += gm_ref[...].astype(F32) * _dot(om_ref[...], pm_ref[...])
    o_ref[...] = acc.astype(o_ref.dtype)


def gated_merge(o_a, o_r, o_m, p_a, p_r, p_m, gates, bm=1024, bn=1024):
    m, k = o_a.shape
    n = p_a.shape[1]
    bm, bn = min(bm, m), min(bn, n)
    nb = n // bn
    a_spec = pl.BlockSpec((bm, k), lambda i, j: (i, 0))
    w_spec = pl.BlockSpec((k, bn), lambda i, j: (0, j))
    gate = lambda o: pl.BlockSpec((bm, bn), lambda i, j: (i, j + o * nb))
    return pl.pallas_call(
        _merge_kernel,
        out_shape=jax.ShapeDtypeStruct((m, n), BF16),
        grid=(m // bm, nb),
        in_specs=[a_spec, a_spec, a_spec, w_spec, w_spec, w_spec, gate(0), gate(1), gate(2)],
        out_specs=pl.BlockSpec((bm, bn), lambda i, j: (i, j)),
        compiler_params=_params(("parallel", "arbitrary"), 56),
        name="gated_merge",
    )(o_a, o_r, o_m, p_a, p_r, p_m, gates, gates, gates)


def _ffn_up_kernel(h_ref, hp_ref, hn_ref, wg_ref, wv_ref, cg_ref, cv_ref, bg_ref, bv_ref, o_ref, lhs_ref):
    j = pl.program_id(1)
    nt = pl.num_programs(1)
    bm = h_ref.shape[1]
    halo = hp_ref.shape[1]

    @pl.when(pl.program_id(2) == 0)
    def _():
        lhs_ref[0:bm] = h_ref[0]
        lhs_ref[bm:bm + halo] = hp_ref[0]
        lhs_ref[bm + halo:] = hn_ref[0]

    def branch(w_ref, c_ref, b_ref):
        u_all = _dot(lhs_ref[...], w_ref[...])
        u = u_all[:bm]
        prev_row = jnp.where(j > 0, u_all[bm + halo - 1:bm + halo], 0.0)
        next_row = jnp.where(j < nt - 1, u_all[bm + halo:bm + halo + 1], 0.0)
        up, dn = _shift_rows(u, prev_row, next_row)
        c = c_ref[...]
        return c[0:1] * up + c[1:2] * u + c[2:3] * dn + b_ref[...]

    gt = branch(wg_ref, cg_ref, bg_ref)
    val = branch(wv_ref, cv_ref, bv_ref)
    o_ref[0] = (gt * jax.nn.sigmoid(gt) * val).astype(o_ref.dtype)


def ffn_up(h, w_up, conv_w, conv_b, bm=1024, bn=512):
    b, t, k = h.shape
    bm = min(bm, t)
    nb = D_FF // bn
    halo = 16
    hb = bm // halo
    nh = t // halo
    return pl.pallas_call(
        _ffn_up_kernel,
        out_shape=jax.ShapeDtypeStruct((b, t, D_FF), BF16),
        grid=(b, t // bm, nb),
        in_specs=[pl.BlockSpec((1, bm, k), lambda i, j, n: (i, j, 0)),
                  pl.BlockSpec((1, halo, k), lambda i, j, n: (i, jnp.maximum(j * hb - 1, 0), 0)),
                  pl.BlockSpec((1, halo, k), lambda i, j, n: (i, jnp.minimum((j + 1) * hb, nh - 1), 0)),
                  pl.BlockSpec((k, bn), lambda i, j, n: (0, n)),
                  pl.BlockSpec((k, bn), lambda i, j, n: (0, n + nb)),
                  pl.BlockSpec((3, bn), lambda i, j, n: (0, n)),
                  pl.BlockSpec((3, bn), lambda i, j, n: (0, n + nb)),
                  pl.BlockSpec((1, bn), lambda i, j, n: (0, n)),
                  pl.BlockSpec((1, bn), lambda i, j, n: (0, n + nb))],
        out_specs=pl.BlockSpec((1, bm, bn), lambda i, j, n: (i, j, n)),
        scratch_shapes=[pltpu.VMEM((bm + 2 * halo, k), BF16)],
        compiler_params=_params(("parallel", "parallel", "arbitrary"), 56),
        name="ffn_up",
    )(h, h, h, w_up, w_up, conv_w, conv_w, conv_b.reshape(1, -1), conv_b.reshape(1, -1))


def _rope_tables(t):
    half = A_HEAD_DIM // 2
    inv = jnp.power(ROPE_THETA, -jnp.arange(half, dtype=F32) / half)
    ang = jnp.arange(t, dtype=F32)[:, None] * inv[None, :]
    reps = LANES // half
    return jnp.tile(jnp.cos(ang), (1, reps)), jnp.tile(jnp.sin(ang), (1, reps))


def _layer(x, mem, p, w, layer_idx):
    b, t, d = x.shape
    m = b * t
    lam_init = 0.8 - 0.6 * math.exp(-0.3 * layer_idx)
    x2 = x.reshape(m, d)
    h = rmsnorm_rows(x2, p["g_mix"])
    zm = matmul(h, w["w_m"], BF16).reshape(b, t, -1)
    zqkv = matmul(h, w["w_qkv"], BF16)
    zr, qk = matmul_with_qk_prep(h, w["w_r"], zqkv, t, p["q_norm"], p["k_norm"])
    gates, r, v, kk, lw, kd, bb, g, bonus = gates_with_rwkv_prep(
        h, w["w_g"], zr, t, p["mu_prev"], p["mu_next"], p["decay_w0"], p["decay_w2"], p["aaa_a0"],
        p["aaa_a2"], p["gate_g2"], p["k_k"], p["k_a"], p["r_k"])
    y_fwd, y_bwd = rwkv_scan(r, v, kk, lw, kd, bb, t)
    o_r = rwkv_post(y_fwd, y_bwd, bonus, g, p["lnx_w"], p["lnx_b"])
    hm = rmsnorm_rows(mem.reshape(b * N_MEM, d), p["g_memnorm"])
    kv = matmul(hm, w["w_mem_kv"], F32).reshape(b, N_MEM, -1)
    o_m = memory_attention(zm, kv, p["mq_norm"], p["mk_norm"])
    o_a = diff_attention(qk.reshape(b, t, -1), zqkv.reshape(b, t, -1), p["lambda_q1"], p["lambda_k1"],
                         p["lambda_q2"], p["lambda_k2"], p["attn_subln"], lam_init)
    merged = gated_merge(o_a.reshape(m, -1), o_r, o_m.reshape(m, -1),
                         w["p_attn"], w["p_rwkv"], w["p_mem"], gates)
    x2, h2 = matmul_residual_norm(merged, w["w_o"], x2, p["g_ffn"])
    act = ffn_up(h2.reshape(b, t, d), w["w_up"], p["conv_w"], p["conv_b"])
    x2 = matmul_residual(act.reshape(m, -1), w["w_down"], x2)
    return x2.reshape(b, t, d)


def _prepare_weights(p):
    w_in = p["w_in"]
    c0 = 3 * A_WIDTH
    c1 = c0 + R_COLS
    c2 = c1 + M_WIDTH
    return {
        "w_qkv": w_in[:, :c0].astype(BF16),
        "w_r": jnp.pad(w_in[:, c0:c1], ((0, 0), (0, R_COLS_PAD - R_COLS))).astype(BF16),
        "w_m": w_in[:, c1:c2].astype(BF16),
        "w_g": w_in[:, c2:].astype(BF16),
        "w_mem_kv": p["w_mem_kv"].astype(BF16),
        "p_attn": p["p_attn"].astype(BF16),
        "p_rwkv": p["p_rwkv"].astype(BF16),
        "p_mem": p["p_mem"].astype(BF16),
        "w_o": p["w_o"].astype(BF16),
        "w_up": p["w_up"].astype(BF16),
        "w_down": p["w_down"].astype(BF16),
    }


def kernel(x_prompt, x_sample, mem_prompt, mem_sample, g_mix, w_in, q_norm, k_norm, lambda_q1, lambda_k1, lambda_q2, lambda_k2, attn_subln, mu_prev, mu_next, decay_w0, decay_w2, aaa_a0, aaa_a2, gate_g2, k_k, k_a, r_k, lnx_w, lnx_b, g_memnorm, w_mem_kv, mq_norm, mk_norm, p_attn, p_rwkv, p_mem, w_o, g_ffn, w_up, conv_w, conv_b, w_down):
    params = dict(g_mix=g_mix, w_in=w_in, q_norm=q_norm, k_norm=k_norm, lambda_q1=lambda_q1,
                  lambda_k1=lambda_k1, lambda_q2=lambda_q2, lambda_k2=lambda_k2, attn_subln=attn_subln,
                  mu_prev=mu_prev, mu_next=mu_next, decay_w0=decay_w0, decay_w2=decay_w2, aaa_a0=aaa_a0,
                  aaa_a2=aaa_a2, gate_g2=gate_g2, k_k=k_k, k_a=k_a, r_k=r_k, lnx_w=lnx_w, lnx_b=lnx_b,
                  g_memnorm=g_memnorm, w_mem_kv=w_mem_kv, mq_norm=mq_norm, mk_norm=mk_norm, p_attn=p_attn,
                  p_rwkv=p_rwkv, p_mem=p_mem, w_o=w_o, g_ffn=g_ffn, w_up=w_up, conv_w=conv_w, conv_b=conv_b,
                  w_down=w_down)
    layers = [{name: arr[l] for name, arr in params.items()} for l in range(g_mix.shape[0])]
    weights = [_prepare_weights(p) for p in layers]
    outs = []
    for x, mem in ((x_prompt, mem_prompt), (x_sample, mem_sample)):
        for l, (p, w) in enumerate(zip(layers, weights)):
            x = _layer(x, mem, p, w, l)
        outs.append(x)
    return tuple(outs)
```

```python
import functools
import math

import jax
import jax.numpy as jnp
from jax import lax
from jax.experimental import pallas as pl
from jax.experimental.pallas import tpu as pltpu

F32 = jnp.float32
BF16 = jnp.bfloat16

D_MODEL = 2048
A_HEADS = 8
A_HEAD_DIM = 64
A_WIDTH = 1024
ROPE_THETA = 10000.0
R_HEAD = 64
R_WIDTH = 1024
R_COLS = 3488
R_COLS_PAD = 3584
GATE_RANK = 160
GATE_RANK_PAD = 256
LNX_EPS = 64e-5
N_MEM = 256
M_HEADS = 4
M_HEAD_DIM = 256
M_WIDTH = 1024
D_FF = 5632
EPS = 1e-6
LANES = 128
CHUNK = 64
INV_BASE_BLOCK = 4
LOG2E = 1.4426950408889634

NN = (((1,), (0,)), ((), ()))
NT = (((1,), (1,)), ((), ()))
TN = (((0,), (0,)), ((), ()))


def _params(sem, vmem_mb):
    return pltpu.CompilerParams(dimension_semantics=sem, vmem_limit_bytes=vmem_mb << 20)


def _dot(a, b, dims=NN):
    return lax.dot_general(a, b, dims, preferred_element_type=F32)


def _rmsnorm_kernel(x_ref, g_ref, o_ref):
    x = x_ref[...]
    ms = jnp.mean(x * x, axis=-1, keepdims=True)
    o_ref[...] = (x * lax.rsqrt(ms + EPS) * g_ref[...]).astype(o_ref.dtype)


def rmsnorm_rows(x2, g, bm=512):
    m, d = x2.shape
    bm = min(bm, m)
    return pl.pallas_call(
        _rmsnorm_kernel,
        out_shape=jax.ShapeDtypeStruct((m, d), BF16),
        grid=(m // bm,),
        in_specs=[pl.BlockSpec((bm, d), lambda i: (i, 0)),
                  pl.BlockSpec((1, d), lambda i: (0, 0))],
        out_specs=pl.BlockSpec((bm, d), lambda i: (i, 0)),
        compiler_params=_params(("parallel",), 40),
        name="rmsnorm_rows",
    )(x2, g.reshape(1, d))


def _mm_kernel(a_ref, b_ref, o_ref, *, act):
    acc = _dot(a_ref[...], b_ref[...])
    if act == "sigmoid":
        acc = jax.nn.sigmoid(acc)
    o_ref[...] = acc.astype(o_ref.dtype)


def matmul(a, b, out_dtype, act=None, bm=1024, bn=1024):
    m, k = a.shape
    n = b.shape[1]
    bm, bn = min(bm, m), min(bn, n)
    return pl.pallas_call(
        functools.partial(_mm_kernel, act=act),
        out_shape=jax.ShapeDtypeStruct((m, n), out_dtype),
        grid=(m // bm, n // bn),
        in_specs=[pl.BlockSpec((bm, k), lambda i, j: (i, 0)),
                  pl.BlockSpec((k, bn), lambda i, j: (0, j))],
        out_specs=pl.BlockSpec((bm, bn), lambda i, j: (i, j)),
        compiler_params=_params(("parallel", "arbitrary"), 48),
        name="matmul",
    )(a, b)


def _mm_res_kernel(a_ref, b_ref, r_ref, o_ref):
    o_ref[...] = r_ref[...] + _dot(a_ref[...], b_ref[...])


def matmul_residual(a, b, res, bm=1024, bn=512):
    m, k = a.shape
    n = b.shape[1]
    bm, bn = min(bm, m), min(bn, n)
    return pl.pallas_call(
        _mm_res_kernel,
        out_shape=jax.ShapeDtypeStruct((m, n), F32),
        grid=(m // bm, n // bn),
        in_specs=[pl.BlockSpec((bm, k), lambda i, j: (i, 0)),
                  pl.BlockSpec((k, bn), lambda i, j: (0, j)),
                  pl.BlockSpec((bm, bn), lambda i, j: (i, j))],
        out_specs=pl.BlockSpec((bm, bn), lambda i, j: (i, j)),
        compiler_params=_params(("parallel", "arbitrary"), 56),
        name="matmul_residual",
    )(a, b, res)


def _mm_res_norm_kernel(a_ref, b_ref, r_ref, g_ref, x_ref, h_ref):
    x = r_ref[...] + _dot(a_ref[...], b_ref[...])
    x_ref[...] = x
    ms = jnp.mean(x * x, axis=-1, keepdims=True)
    h_ref[...] = (x * lax.rsqrt(ms + EPS) * g_ref[...]).astype(h_ref.dtype)


def matmul_residual_norm(a, b, res, g, bm=512):
    m, k = a.shape
    n = b.shape[1]
    bm = min(bm, m)
    return pl.pallas_call(
        _mm_res_norm_kernel,
        out_shape=(jax.ShapeDtypeStruct((m, n), F32), jax.ShapeDtypeStruct((m, n), BF16)),
        grid=(m // bm,),
        in_specs=[pl.BlockSpec((bm, k), lambda i: (i, 0)),
                  pl.BlockSpec((k, n), lambda i: (0, 0)),
                  pl.BlockSpec((bm, n), lambda i: (i, 0)),
                  pl.BlockSpec((1, n), lambda i: (0, 0))],
        out_specs=(pl.BlockSpec((bm, n), lambda i: (i, 0)), pl.BlockSpec((bm, n), lambda i: (i, 0))),
        compiler_params=_params(("parallel",), 56),
        name="matmul_residual_norm",
    )(a, b, res, g.reshape(1, n))


def _group_ones(group):
    r = lax.broadcasted_iota(jnp.int32, (2 * LANES, LANES), 0)
    c = lax.broadcasted_iota(jnp.int32, (2 * LANES, LANES), 1)
    return jnp.where((r % LANES) // group == c // group, 1.0, 0.0).astype(BF16)


def _group_sum(x, ones):
    hi = x.astype(BF16)
    lo = (x - hi.astype(F32)).astype(BF16)
    outs = []
    for c in range(x.shape[1] // LANES):
        sl = slice(LANES * c, LANES * (c + 1))
        outs.append(_dot(jnp.concatenate([hi[:, sl], lo[:, sl]], axis=1), ones))
    return outs[0] if len(outs) == 1 else jnp.concatenate(outs, axis=1)


def _tile_lanes(x, reps):
    return x if reps == 1 else jnp.concatenate([x] * reps, axis=1)


def _shift_rows(u, prev_row, next_row):
    n = u.shape[0]
    row = lax.broadcasted_iota(jnp.int32, u.shape, 0)
    up = jnp.where(row == 0, prev_row, pltpu.roll(u, 1, 0))
    dn = jnp.where(row == n - 1, next_row, pltpu.roll(u, n - 1, 0))
    return up, dn


def _norm_rope(x, g, cos, sin, scale):
    reps = x.shape[1] // LANES
    ss = _group_sum(x * x, _group_ones(A_HEAD_DIM))
    xn = x * lax.rsqrt(ss * (1.0 / A_HEAD_DIM) + EPS) * _tile_lanes(g, reps)
    lane = lax.broadcasted_iota(jnp.int32, x.shape, 1)
    half = A_HEAD_DIM // 2
    rot = jnp.where(lane % A_HEAD_DIM < half,
                    -pltpu.roll(xn, x.shape[1] - half, 1), pltpu.roll(xn, half, 1))
    return (xn * _tile_lanes(cos, reps) + rot * _tile_lanes(sin, reps)) * scale


def _mm_qkprep_kernel(a_ref, b_ref, xq_ref, xk_ref, cos_ref, sin_ref, gq_ref, gk_ref, z_ref, qk_ref):
    z_ref[...] = _dot(a_ref[...], b_ref[...]).astype(z_ref.dtype)
    cos, sin = cos_ref[...], sin_ref[...]
    q = _norm_rope(xq_ref[...].astype(F32), gq_ref[...], cos, sin, A_HEAD_DIM ** -0.5 * LOG2E)
    k = _norm_rope(xk_ref[...].astype(F32), gk_ref[...], cos, sin, 1.0)
    qk_ref[:, :A_WIDTH] = q.astype(qk_ref.dtype)
    qk_ref[:, A_WIDTH:] = k.astype(qk_ref.dtype)


def matmul_with_qk_prep(h, w_r, zqkv, t, q_norm, k_norm, bm=1024, nsub=4):
    m, k = h.shape
    n = w_r.shape[1]
    bm = min(bm, m)
    bn, sub = n // nsub, bm // nsub
    nt = t // sub
    cos, sin = _rope_tables(t)
    gq = jnp.tile(q_norm.reshape(1, A_HEAD_DIM), (1, LANES // A_HEAD_DIM))
    gk = jnp.tile(k_norm.reshape(1, A_HEAD_DIM), (1, LANES // A_HEAD_DIM))
    slab = lambda i, j: i * nsub + j
    return pl.pallas_call(
        _mm_qkprep_kernel,
        out_shape=(jax.ShapeDtypeStruct((m, n), BF16), jax.ShapeDtypeStruct((m, 2 * A_WIDTH), BF16)),
        grid=(m // bm, nsub),
        in_specs=[pl.BlockSpec((bm, k), lambda i, j: (i, 0)),
                  pl.BlockSpec((k, bn), lambda i, j: (0, j)),
                  pl.BlockSpec((sub, A_WIDTH), lambda i, j: (slab(i, j), 0)),
                  pl.BlockSpec((sub, A_WIDTH), lambda i, j: (slab(i, j), 1)),
                  pl.BlockSpec((sub, LANES), lambda i, j: (slab(i, j) % nt, 0)),
                  pl.BlockSpec((sub, LANES), lambda i, j: (slab(i, j) % nt, 0)),
                  pl.BlockSpec((1, LANES), lambda i, j: (0, 0)),
                  pl.BlockSpec((1, LANES), lambda i, j: (0, 0))],
        out_specs=(pl.BlockSpec((bm, bn), lambda i, j: (i, j)),
                   pl.BlockSpec((sub, 2 * A_WIDTH), lambda i, j: (slab(i, j), 0))),
        compiler_params=_params(("parallel", "arbitrary"), 56),
        name="matmul_with_qk_prep",
    )(h, w_r, zqkv, zqkv, cos, sin, gq, gk)


def _dattn_kernel(lq1_ref, lk1_ref, lq2_ref, lk2_ref, g_ref, q_ref, k_ref, v_ref, o_ref,
                  m_ref, acc_ref, s_ref, qq_ref, *, bq, bk, group, lam_init):
    t = k_ref.shape[1]
    nk, nq = t // bk, t // bq
    ones = jnp.ones((bk, LANES), BF16)
    lane = lax.broadcasted_iota(jnp.int32, (bq, LANES), 1)
    first = lane < A_HEAD_DIM
    lam = (jnp.exp(jnp.sum(lq1_ref[...] * lk1_ref[...], keepdims=True))
           - jnp.exp(jnp.sum(lq2_ref[...] * lk2_ref[...], keepdims=True)) + lam_init)

    def stack_q(qi, slot):
        start = pl.multiple_of(qi * bq, bq)
        q = q_ref[0, pl.ds(start, bq), :]
        zero = jnp.zeros_like(q)
        qq_ref[slot] = jnp.concatenate([jnp.where(first, q, zero), jnp.where(first, zero, q)], axis=0)

    def scores(slot, j):
        start = pl.multiple_of(j * bk, bk)
        return _dot(qq_ref[slot], k_ref[0, pl.ds(start, bk), :], NT)

    def accumulate(j, s):
        start = pl.multiple_of(j * bk, bk)
        vs = v_ref[0, pl.ds(start, bk), :]
        m_prev = m_ref[...]
        m_new = jnp.maximum(m_prev, jnp.max(s, axis=-1, keepdims=True))
        alpha = jnp.exp2(m_prev - m_new)
        p = jnp.exp2(s - _tile_lanes(m_new, bk // LANES))
        pv = _dot(p.astype(BF16), jnp.concatenate([vs, ones], axis=1))
        acc_ref[...] = _tile_lanes(alpha, 2) * acc_ref[...] + pv
        m_ref[...] = m_new

    def run_group(slot, j0, next_scores):
        for u in range(group):
            if u < group - 1:
                s_ref[(u + 1) % 2] = scores(slot, j0 + u + 1)
            else:
                next_scores()
            accumulate(j0 + u, s_ref[u % 2])

    stack_q(0, 0)
    s_ref[0] = scores(0, 0)

    def q_block(qi, carry):
        slot = qi % 2
        m_ref[...] = jnp.full(m_ref.shape, -jnp.inf, F32)
        acc_ref[...] = jnp.zeros(acc_ref.shape, F32)

        def same_q(j0):
            def issue():
                s_ref[0] = scores(slot, j0 + group)
            return issue

        def body(i, c):
            run_group(slot, group * i, same_q(group * i))
            return c

        lax.fori_loop(0, nk // group - 1, body, 0)

        def next_q():
            stack_q(jnp.minimum(qi + 1, nq - 1), 1 - slot)
            s_ref[0] = scores(1 - slot, 0)

        run_group(slot, nk - group, next_q)
        o12 = acc_ref[:, :LANES] / acc_ref[:, LANES:]
        o = o12[:bq] - lam * o12[bq:]
        ms = jnp.mean(o * o, axis=-1, keepdims=True)
        y = (o * lax.rsqrt(ms + EPS) * g_ref[...]) * (1.0 - lam_init)
        o_ref[0, pl.ds(pl.multiple_of(qi * bq, bq), bq), :] = y.astype(o_ref.dtype)
        return carry

    lax.fori_loop(0, nq, q_block, 0)


def diff_attention(qk, zqkv, lq1, lk1, lq2, lk2, sub_g, lam_init, bq=512, bk=512):
    b, t, _ = qk.shape
    group = 8
    bq, bk = min(bq, t), min(bk, t // group)
    assert (t // bk) % group == 0
    vec = lambda a: a.reshape(1, -1)
    small = lambda n: pl.BlockSpec((1, n), lambda i, h: (0, 0))
    seq = lambda off: pl.BlockSpec((1, t, LANES), lambda i, h: (i, 0, off + h))
    return pl.pallas_call(
        functools.partial(_dattn_kernel, bq=bq, bk=bk, group=group, lam_init=lam_init),
        out_shape=jax.ShapeDtypeStruct((b, t, A_WIDTH), BF16),
        grid=(b, A_HEADS),
        in_specs=[small(A_HEAD_DIM), small(A_HEAD_DIM), small(A_HEAD_DIM), small(A_HEAD_DIM),
                  small(LANES), seq(0), seq(A_HEADS), seq(2 * A_HEADS)],
        out_specs=seq(0),
        scratch_shapes=[pltpu.VMEM((2 * bq, LANES), F32), pltpu.VMEM((2 * bq, 2 * LANES), F32),
                        pltpu.VMEM((2, 2 * bq, bk), F32), pltpu.VMEM((2, 2 * bq, LANES), BF16)],
        compiler_params=_params(("parallel", "arbitrary"), 56),
        name="diff_attention",
    )(vec(lq1), vec(lk1), vec(lq2), vec(lk2), vec(sub_g), qk, qk, zqkv)


def _mem_attn_kernel(q_ref, kv_ref, gq_ref, gk_ref, o_ref):
    outs = []
    for h in range(M_HEADS):
        sl = slice(h * M_HEAD_DIM, (h + 1) * M_HEAD_DIM)
        qh = q_ref[0, :, sl].astype(F32)
        qn = qh * lax.rsqrt(jnp.mean(qh * qh, axis=-1, keepdims=True) + EPS) * gq_ref[...]
        kh = kv_ref[0, :, sl]
        kn = kh * lax.rsqrt(jnp.mean(kh * kh, axis=-1, keepdims=True) + EPS) * gk_ref[...]
        vh = kv_ref[0, :, M_WIDTH + h * M_HEAD_DIM:M_WIDTH + (h + 1) * M_HEAD_DIM]
        s = _dot((qn * M_HEAD_DIM ** -0.5).astype(BF16), kn.astype(BF16), NT)
        p = jnp.exp(s - jnp.max(s, axis=-1, keepdims=True))
        o = _dot(p.astype(BF16), vh.astype(BF16)) / jnp.sum(p, axis=-1, keepdims=True)
        outs.append(o)
    o_ref[0] = jnp.concatenate(outs, axis=1).astype(o_ref.dtype)


def memory_attention(zm, kv, mq_norm, mk_norm, bt=512):
    b, t, _ = zm.shape
    bt = min(bt, t)
    return pl.pallas_call(
        _mem_attn_kernel,
        out_shape=jax.ShapeDtypeStruct((b, t, M_WIDTH), BF16),
        grid=(b, t // bt),
        in_specs=[pl.BlockSpec((1, bt, M_WIDTH), lambda i, j: (i, j, 0)),
                  pl.BlockSpec((1, N_MEM, 2 * M_WIDTH), lambda i, j: (i, 0, 0)),
                  pl.BlockSpec((1, M_HEAD_DIM), lambda i, j: (0, 0)),
                  pl.BlockSpec((1, M_HEAD_DIM), lambda i, j: (0, 0))],
        out_specs=pl.BlockSpec((1, bt, M_WIDTH), lambda i, j: (i, j, 0)),
        compiler_params=_params(("parallel", "arbitrary"), 40),
        name="memory_attention",
    )(zm, kv, mq_norm.reshape(1, -1), mk_norm.reshape(1, -1))


def _mm_rwkvprep_kernel(a_ref, b_ref, z_ref, zp_ref, zn_ref, mup_ref, mun_ref, w0_ref, w2_ref, a0_ref, a2_ref,
                        g2_ref, kk_w_ref, ka_ref, rk_ref,
                        gates_ref, r_ref, v_ref, kk_ref, lw_ref, kd_ref, bb_ref, g_ref, bonus_ref, *, nt):
    n_chunks = 3
    cw = b_ref.shape[1] // n_chunks

    def gate_chunk(c):
        cols = slice(c * cw, (c + 1) * cw)
        gates_ref[:, cols] = jax.nn.sigmoid(_dot(a_ref[...], b_ref[:, cols])).astype(gates_ref.dtype)

    gate_chunk(0)
    j = (pl.program_id(0) * pl.num_programs(1) + pl.program_id(1)) % nt
    z = z_ref[...].astype(F32)
    last = zp_ref.shape[0] - 1
    prev_row = jnp.where(j > 0, zp_ref[last:last + 1, :].astype(F32), 0.0)
    next_row = jnp.where(j < nt - 1, zn_ref[0:1, :].astype(F32), 0.0)
    zp, zn = _shift_rows(z, prev_row, next_row)
    z = z + mup_ref[...] * (zp - z) + mun_ref[...] * (zn - z)
    w = R_WIDTH
    r, k, v = z[:, 0:w], z[:, w:2 * w], z[:, 2 * w:3 * w]
    zw = z[:, 3 * w:3 * w + LANES]
    za = z[:, 3 * w + LANES:3 * w + 2 * LANES]
    zg = z[:, 3 * w + 2 * LANES:]
    r_ref[...] = r.astype(r_ref.dtype)
    v_ref[...] = v.astype(v_ref.dtype)
    wlin = w0_ref[...] + _dot(jnp.tanh(zw).astype(BF16), w2_ref[...])
    alin = a0_ref[...] + _dot(za.astype(BF16), a2_ref[...])
    g = _dot(jax.nn.sigmoid(zg).astype(BF16), g2_ref[...])
    g_ref[...] = g.astype(g_ref.dtype)
    ones = _group_ones(R_HEAD)
    kk = k * kk_w_ref[...]
    kk_sq = _group_sum(kk * kk, ones)
    gate_chunk(1)
    lw = -math.exp(-0.5) * jax.nn.sigmoid(wlin)
    lw_ref[0] = lw[:, :w]
    lw_ref[1] = lw[:, w:]
    a = jax.nn.sigmoid(alin)
    kk = kk / jnp.maximum(jnp.sqrt(kk_sq), 1e-12)
    kk_ref[...] = kk.astype(kk_ref.dtype)
    bb_ref[0] = (kk * a[:, :w]).astype(bb_ref.dtype)
    bb_ref[1] = (kk * a[:, w:]).astype(bb_ref.dtype)
    gate_chunk(2)
    ka = ka_ref[...]
    kd0 = k * (1.0 + (a[:, :w] - 1.0) * ka)
    kd1 = k * (1.0 + (a[:, w:] - 1.0) * ka)
    kd_ref[0] = kd0.astype(kd_ref.dtype)
    kd_ref[1] = kd1.astype(kd_ref.dtype)
    bonus_sum = _group_sum(r * (kd0 + kd1) * rk_ref[...], ones)
    bonus_ref[...] = (bonus_sum * v).astype(bonus_ref.dtype)


def gates_with_rwkv_prep(h, w_g, zr, t, mu_prev, mu_next, decay_w0, decay_w2, aaa_a0, aaa_a2, gate_g2,
                         k_k, k_a, r_k, bm=1024, nsub=4):
    m, k = h.shape
    n = w_g.shape[1]
    cols = zr.shape[1]
    bm = min(bm, m)
    bn, sub = n // nsub, bm // nsub
    w = R_WIDTH
    pad = cols - R_COLS
    mup = jnp.pad(mu_prev, (0, pad)).reshape(1, cols)
    mun = jnp.pad(mu_next, (0, pad)).reshape(1, cols)

    def blockdiag(m):
        z = jnp.zeros_like(m[0])
        return jnp.concatenate([jnp.concatenate([m[0], z], axis=1),
                                jnp.concatenate([z, m[1]], axis=1)], axis=0).astype(BF16)

    g2 = jnp.pad(gate_g2, ((0, GATE_RANK_PAD - GATE_RANK), (0, 0))).astype(BF16)
    row = lambda n: pl.BlockSpec((1, n), lambda i, j: (0, 0))
    full = lambda r_, c_: pl.BlockSpec((r_, c_), lambda i, j: (0, 0))
    halo = 16
    hb = sub // halo
    nh = m // halo
    slab = lambda i, j: i * nsub + j
    tok = pl.BlockSpec((sub, w), lambda i, j: (slab(i, j), 0))
    tok2 = pl.BlockSpec((2, sub, w), lambda i, j: (0, slab(i, j), 0))
    s1 = jax.ShapeDtypeStruct((m, w), BF16)
    s2 = jax.ShapeDtypeStruct((2, m, w), BF16)
    lw_shape = jax.ShapeDtypeStruct((2, m, w), F32)
    return pl.pallas_call(
        functools.partial(_mm_rwkvprep_kernel, nt=t // sub),
        out_shape=(jax.ShapeDtypeStruct((m, n), BF16), s1, s1, s1, lw_shape, s2, s2, s1, s1),
        grid=(m // bm, nsub),
        in_specs=[pl.BlockSpec((bm, k), lambda i, j: (i, 0)),
                  pl.BlockSpec((k, bn), lambda i, j: (0, j)),
                  pl.BlockSpec((sub, cols), lambda i, j: (slab(i, j), 0)),
                  pl.BlockSpec((halo, cols), lambda i, j: (jnp.maximum(slab(i, j) * hb - 1, 0), 0)),
                  pl.BlockSpec((halo, cols), lambda i, j: (jnp.minimum((slab(i, j) + 1) * hb, nh - 1), 0)),
                  row(cols), row(cols), row(2 * w), full(LANES, 2 * w), row(2 * w), full(LANES, 2 * w),
                  full(GATE_RANK_PAD, w), row(w), row(w), row(w)],
        out_specs=(pl.BlockSpec((bm, bn), lambda i, j: (i, j)), tok, tok, tok, tok2, tok2, tok2, tok, tok),
        compiler_params=_params(("parallel", "arbitrary"), 58),
        name="gates_with_rwkv_prep",
    )(h, w_g, zr, zr, zr, mup, mun, decay_w0.reshape(1, 2 * w), blockdiag(decay_w2), aaa_a0.reshape(1, 2 * w),
      blockdiag(aaa_a2), g2, k_k.reshape(1, w), k_a.reshape(1, w), r_k.reshape(1, w))


def _split3(x):
    hi = x.astype(BF16)
    r1 = x - hi.astype(F32)
    mid = r1.astype(BF16)
    lo = (r1 - mid.astype(F32)).astype(BF16)
    return hi, mid, lo


def _rwkv_scan_kernel(rf_ref, vf_ref, kkf_ref, lwf_ref, kdf_ref, bbf_ref,
                      rb_ref, vb_ref, kkb_ref, lwb_ref, kdb_ref, bbb_ref, yf_ref, yb_ref, s_ref):
    C = CHUNK
    n_pairs = R_WIDTH // LANES
    assert C == R_HEAD

    @pl.when(pl.program_id(1) == 0)
    def _():
        s_ref[...] = jnp.zeros(s_ref.shape, F32)

    ti = lax.broadcasted_iota(jnp.int32, (C, C), 0)
    si = lax.broadcasted_iota(jnp.int32, (C, C), 1)
    r2 = lax.broadcasted_iota(jnp.int32, (2 * C, 2 * C), 0)
    c2 = lax.broadcasted_iota(jnp.int32, (2 * C, 2 * C), 1)
    same_head = (r2 // C) == (c2 // C)
    lane = lax.broadcasted_iota(jnp.int32, (C, LANES), 1)
    row = lax.broadcasted_iota(jnp.int32, (C, LANES), 0)
    first = lane < R_HEAD
    src = lane % C
    eye_h = jnp.where(row == src, 1.0, 0.0)

    def stack(x):
        return jnp.concatenate([jnp.where(first, x, 0.0), jnp.where(first, 0.0, x)], axis=0)

    n_sub = rf_ref.shape[0] // C
    chains = []
    for d, (r_ref, v_ref, kk_ref, lw_ref, kd_ref, bb_ref, y_ref) in enumerate(
            ((rf_ref, vf_ref, kkf_ref, lwf_ref, kdf_ref, bbf_ref, yf_ref),
             (rb_ref, vb_ref, kkb_ref, lwb_ref, kdb_ref, bbb_ref, yb_ref))):
        sign = 1 if d == 0 else -1
        strict_h = sign * (row - src) > 0
        incl_h = sign * (row - src) >= 0
        tri = jnp.where(sign * (ti - si) >= 0, 1.0, 0.0).astype(BF16)
        for q in range(n_sub):
            k = q if d == 0 else n_sub - 1 - q
            rows = slice(k * C, (k + 1) * C)
            lw_all = lw_ref[0, rows, :]
            hi, mid, lo = _split3(lw_all)
            cum_all = _dot(tri, hi) + (_dot(tri, mid) + _dot(tri, lo))
            tot_all = jnp.sum(lw_all, axis=0, keepdims=True)
            for p in range(n_pairs):
                sl = slice(p * LANES, (p + 1) * LANES)
                chains.append(dict(d=d, q=q, p=p, sl=sl, rows=rows, strict_h=strict_h, incl_h=incl_h,
                                   lw=lw_all[:, sl], cum=cum_all[:, sl], tot=tot_all[:, sl],
                                   r=r_ref, v=v_ref, kk=kk_ref, kd=kd_ref, bb=bb_ref, y=y_ref))

    for ch in chains:
        sl, rows, lw, cum, tot = ch["sl"], ch["rows"], ch["lw"], ch["cum"], ch["tot"]
        r, v, kk = (ch[n][rows, sl].astype(F32) for n in ("r", "v", "kk"))
        kd, bb = (ch[n][0, rows, sl].astype(F32) for n in ("kd", "bb"))
        e_neg = jnp.exp(-cum)
        e_end = jnp.exp(tot - cum)
        at = -kk * jnp.exp(cum - lw)
        rt = r * jnp.exp(cum)
        ch["ar"] = jnp.concatenate([at, rt], axis=0).astype(BF16)
        ch["bk"] = jnp.concatenate([stack(bb * e_neg), stack(kd * e_neg)], axis=0).astype(BF16)
        ch["bk_end"] = jnp.concatenate([bb * e_end, kd * e_end], axis=0).astype(BF16)
        ch["v"] = v.astype(BF16)
    for ch in chains:
        ch["pm"] = _dot(ch["ar"], ch["bk"], NT)
    for ch in chains:
        pm = ch["pm"]
        ch["lak_mrk"] = jnp.concatenate([jnp.where(ch["strict_h"], pm[:C, 2 * C:], 0.0),
                                         jnp.where(ch["incl_h"], pm[C:, 2 * C:], 0.0)],
                                        axis=0).astype(BF16)
        ch["mrb"] = jnp.where(ch["incl_h"], pm[C:, :2 * C], 0.0).astype(BF16)
        ch["lh"] = jnp.where(ch["strict_h"], pm[:C, :2 * C], 0.0)
    for ch in chains:
        ch["lv"] = _dot(ch["lak_mrk"], stack(ch["v"]))
    def product(left, right):
        return _dot(left.astype(BF16), stack(right).astype(BF16))

    blk = INV_BASE_BLOCK
    diag = row // blk == src // blk
    for ch in chains:
        ch["dg"] = jnp.where(diag, ch["lh"], 0.0)
    for ch in chains:
        ch["dg2"] = product(ch["dg"], ch["dg"])
    for ch in chains:
        ch["tm"] = (eye_h + ch["dg"]) + product(eye_h + ch["dg"], ch["dg2"])
    while blk < C:
        couple = (row // (2 * blk) == src // (2 * blk)) & (row // blk != src // blk)
        for ch in chains:
            ch["x"] = product(jnp.where(couple, ch["lh"], 0.0), ch["tm"])
        for ch in chains:
            ch["tm"] = ch["tm"] + product(ch["tm"], ch["x"])
        blk *= 2
    state = {(d, p): s_ref[d, p] for d in range(2) for p in range(n_pairs)}
    for q in range(n_sub):
        now = [ch for ch in chains if ch["q"] == q]
        for ch in now:
            ch["a_s"] = _dot(ch["ar"], state[ch["d"], ch["p"]].astype(BF16), NT)
        for ch in now:
            rhs = ch["a_s"][:C] + ch["lv"][:C]
            ch["ub"] = product(ch["tm"], rhs).astype(BF16)
        for ch in now:
            ch["y"][ch["rows"], ch["sl"]] = ch["a_s"][C:] + ch["lv"][C:] + _dot(ch["mrb"], stack(ch["ub"]))
        for ch in now:
            uv = jnp.concatenate([ch["ub"], ch["v"]], axis=0)
            grow = jnp.where(same_head, _dot(uv, ch["bk_end"], TN), 0.0)
            state[ch["d"], ch["p"]] = state[ch["d"], ch["p"]] * jnp.exp(ch["tot"]) + grow
    for (d, p), s in state.items():
        s_ref[d, p] = s


def rwkv_scan(r, v, kk, lw, kd, bb, t):
    m, w = r.shape
    rows = 2 * CHUNK
    nc = t // rows
    fwd = pl.BlockSpec((rows, w), lambda i, c: (i * nc + c, 0))
    bwd = pl.BlockSpec((rows, w), lambda i, c: (i * nc + nc - 1 - c, 0))
    fwd2 = pl.BlockSpec((1, rows, w), lambda i, c: (0, i * nc + c, 0))
    bwd2 = pl.BlockSpec((1, rows, w), lambda i, c: (1, i * nc + nc - 1 - c, 0))
    out = jax.ShapeDtypeStruct((m, w), F32)
    return pl.pallas_call(
        _rwkv_scan_kernel,
        out_shape=(out, out),
        grid=(m // t, nc),
        in_specs=[fwd, fwd, fwd, fwd2, fwd2, fwd2, bwd, bwd, bwd, bwd2, bwd2, bwd2],
        out_specs=(fwd, bwd),
        scratch_shapes=[pltpu.VMEM((2, w // LANES, LANES, LANES), F32)],
        compiler_params=_params(("parallel", "arbitrary"), 48),
        name="rwkv_scan",
    )(r, v, kk, lw, kd, bb, r, v, kk, lw, kd, bb)


def _rwkv_post_kernel(yf_ref, yb_ref, bonus_ref, g_ref, w_ref, b_ref, o_ref):
    y = yf_ref[...] + yb_ref[...]
    ones = _group_ones(R_HEAD)
    mu = _group_sum(y, ones) * (1.0 / R_HEAD)
    dlt = y - mu
    var = _group_sum(dlt * dlt, ones) * (1.0 / R_HEAD)
    yn = dlt * lax.rsqrt(var + LNX_EPS) * w_ref[...] + b_ref[...]
    o_ref[...] = ((yn + bonus_ref[...].astype(F32)) * g_ref[...].astype(F32)).astype(o_ref.dtype)


def rwkv_post(y_fwd, y_bwd, bonus, g, lnx_w, lnx_b, bt=512):
    m, w = y_fwd.shape
    bt = min(bt, m)
    tok = pl.BlockSpec((bt, w), lambda i: (i, 0))
    row = pl.BlockSpec((1, w), lambda i: (0, 0))
    return pl.pallas_call(
        _rwkv_post_kernel,
        out_shape=jax.ShapeDtypeStruct((m, w), BF16),
        grid=(m // bt,),
        in_specs=[tok, tok, tok, tok, row, row],
        out_specs=tok,
        compiler_params=_params(("parallel",), 40),
        name="rwkv_post",
    )(y_fwd, y_bwd, bonus, g, lnx_w.reshape(1, w), lnx_b.reshape(1, w))


def _merge_kernel(oa_ref, or_ref, om_ref, pa_ref, pr_ref, pm_ref, ga_ref, gr_ref, gm_ref, o_ref):
    acc = ga_ref[...].astype(F32) * _dot(oa_ref[...], pa_ref[...])
    acc += gr_ref[...].astype(F32) * _dot(or_ref[...], pr_ref[...])
    acc += gm_ref[...].astype(F32) * _dot(om_ref[...], pm_ref[...])
    o_ref[...] = acc.astype(o_ref.dtype)


def gated_merge(o_a, o_r, o_m, p_a, p_r, p_m, gates, bm=1024, bn=1024):
    m, k = o_a.shape
    n = p_a.shape[1]
    bm, bn = min(bm, m), min(bn, n)
    nb = n // bn
    a_spec = pl.BlockSpec((bm, k), lambda i, j: (i, 0))
    w_spec = pl.BlockSpec((k, bn), lambda i, j: (0, j))
    gate = lambda o: pl.BlockSpec((bm, bn), lambda i, j: (i, j + o * nb))
    return pl.pallas_call(
        _merge_kernel,
        out_shape=jax.ShapeDtypeStruct((m, n), BF16),
        grid=(m // bm, nb),
        in_specs=[a_spec, a_spec, a_spec, w_spec, w_spec, w_spec, gate(0), gate(1), gate(2)],
        out_specs=pl.BlockSpec((bm, bn), lambda i, j: (i, j)),
        compiler_params=_params(("parallel", "arbitrary"), 56),
        name="gated_merge",
    )(o_a, o_r, o_m, p_a, p_r, p_m, gates, gates, gates)


def _ffn_up_kernel(h_ref, hp_ref, hn_ref, wg_ref, wv_ref, cg_ref, cv_ref, bg_ref, bv_ref, o_ref, lhs_ref):
    j = pl.program_id(1)
    nt = pl.num_programs(1)
    bm = h_ref.shape[1]
    halo = hp_ref.shape[1]

    @pl.when(pl.program_id(2) == 0)
    def _():
        lhs_ref[0:bm] = h_ref[0]
        lhs_ref[bm:bm + halo] = hp_ref[0]
        lhs_ref[bm + halo:] = hn_ref[0]

    def branch(w_ref, c_ref, b_ref):
        u_all = _dot(lhs_ref[...], w_ref[...])
        u = u_all[:bm]
        prev_row = jnp.where(j > 0, u_all[bm + halo - 1:bm + halo], 0.0)
        next_row = jnp.where(j < nt - 1, u_all[bm + halo:bm + halo + 1], 0.0)
        up, dn = _shift_rows(u, prev_row, next_row)
        c = c_ref[...]
        return c[0:1] * up + c[1:2] * u + c[2:3] * dn + b_ref[...]

    gt = branch(wg_ref, cg_ref, bg_ref)
    val = branch(wv_ref, cv_ref, bv_ref)
    o_ref[0] = (gt * jax.nn.sigmoid(gt) * val).astype(o_ref.dtype)


def ffn_up(h, w_up, conv_w, conv_b, bm=1024, bn=512):
    b, t, k = h.shape
    bm = min(bm, t)
    nb = D_FF // bn
    halo = 16
    hb = bm // halo
    nh = t // halo
    return pl.pallas_call(
        _ffn_up_kernel,
        out_shape=jax.ShapeDtypeStruct((b, t, D_FF), BF16),
        grid=(b, t // bm, nb),
        in_specs=[pl.BlockSpec((1, bm, k), lambda i, j, n: (i, j, 0)),
                  pl.BlockSpec((1, halo, k), lambda i, j, n: (i, jnp.maximum(j * hb - 1, 0), 0)),
                  pl.BlockSpec((1, halo, k), lambda i, j, n: (i, jnp.minimum((j + 1) * hb, nh - 1), 0)),
                  pl.BlockSpec((k, bn), lambda i, j, n: (0, n)),
                  pl.BlockSpec((k, bn), lambda i, j, n: (0, n + nb)),
                  pl.BlockSpec((3, bn), lambda i, j, n: (0, n)),
                  pl.BlockSpec((3, bn), lambda i, j, n: (0, n + nb)),
                  pl.BlockSpec((1, bn), lambda i, j, n: (0, n)),
                  pl.BlockSpec((1, bn), lambda i, j, n: (0, n + nb))],
        out_specs=pl.BlockSpec((1, bm, bn), lambda i, j, n: (i, j, n)),
        scratch_shapes=[pltpu.VMEM((bm + 2 * halo, k), BF16)],
        compiler_params=_params(("parallel", "parallel", "arbitrary"), 56),
        name="ffn_up",
    )(h, h, h, w_up, w_up, conv_w, conv_w, conv_b.reshape(1, -1), conv_b.reshape(1, -1))


def _rope_tables(t):
    half = A_HEAD_DIM // 2
    inv = jnp.power(ROPE_THETA, -jnp.arange(half, dtype=F32) / half)
    ang = jnp.arange(t, dtype=F32)[:, None] * inv[None, :]
    reps = LANES // half
    return jnp.tile(jnp.cos(ang), (1, reps)), jnp.tile(jnp.sin(ang), (1, reps))


def _layer(x, mem, p, w, layer_idx):
    b, t, d = x.shape
    m = b * t
    lam_init = 0.8 - 0.6 * math.exp(-0.3 * layer_idx)
    x2 = x.reshape(m, d)
    h = rmsnorm_rows(x2, p["g_mix"])
    zm = matmul(h, w["w_m"], BF16).reshape(b, t, -1)
    zqkv = matmul(h, w["w_qkv"], BF16)
    zr, qk = matmul_with_qk_prep(h, w["w_r"], zqkv, t, p["q_norm"], p["k_norm"])
    gates, r, v, kk, lw, kd, bb, g, bonus = gates_with_rwkv_prep(
        h, w["w_g"], zr, t, p["mu_prev"], p["mu_next"], p["decay_w0"], p["decay_w2"], p["aaa_a0"],
        p["aaa_a2"], p["gate_g2"], p["k_k"], p["k_a"], p["r_k"])
    y_fwd, y_bwd = rwkv_scan(r, v, kk, lw, kd, bb, t)
    o_r = rwkv_post(y_fwd, y_bwd, bonus, g, p["lnx_w"], p["lnx_b"])
    hm = rmsnorm_rows(mem.reshape(b * N_MEM, d), p["g_memnorm"])
    kv = matmul(hm, w["w_mem_kv"], F32).reshape(b, N_MEM, -1)
    o_m = memory_attention(zm, kv, p["mq_norm"], p["mk_norm"])
    o_a = diff_attention(qk.reshape(b, t, -1), zqkv.reshape(b, t, -1), p["lambda_q1"], p["lambda_k1"],
                         p["lambda_q2"], p["lambda_k2"], p["attn_subln"], lam_init)
    merged = gated_merge(o_a.reshape(m, -1), o_r, o_m.reshape(m, -1),
                         w["p_attn"], w["p_rwkv"], w["p_mem"], gates)
    x2, h2 = matmul_residual_norm(merged, w["w_o"], x2, p["g_ffn"])
    act = ffn_up(h2.reshape(b, t, d), w["w_up"], p["conv_w"], p["conv_b"])
    x2 = matmul_residual(act.reshape(m, -1), w["w_down"], x2)
    return x2.reshape(b, t, d)


def _prepare_weights(p):
    w_in = p["w_in"]
    c0 = 3 * A_WIDTH
    c1 = c0 + R_COLS
    c2 = c1 + M_WIDTH
    return {
        "w_qkv": w_in[:, :c0].astype(BF16),
        "w_r": jnp.pad(w_in[:, c0:c1], ((0, 0), (0, R_COLS_PAD - R_COLS))).astype(BF16),
        "w_m": w_in[:, c1:c2].astype(BF16),
        "w_g": w_in[:, c2:].astype(BF16),
        "w_mem_kv": p["w_mem_kv"].astype(BF16),
        "p_attn": p["p_attn"].astype(BF16),
        "p_rwkv": p["p_rwkv"].astype(BF16),
        "p_mem": p["p_mem"].astype(BF16),
        "w_o": p["w_o"].astype(BF16),
        "w_up": p["w_up"].astype(BF16),
        "w_down": p["w_down"].astype(BF16),
    }


def kernel(x_prompt, x_sample, mem_prompt, mem_sample, g_mix, w_in, q_norm, k_norm, lambda_q1, lambda_k1, lambda_q2, lambda_k2, attn_subln, mu_prev, mu_next, decay_w0, decay_w2, aaa_a0, aaa_a2, gate_g2, k_k, k_a, r_k, lnx_w, lnx_b, g_memnorm, w_mem_kv, mq_norm, mk_norm, p_attn, p_rwkv, p_mem, w_o, g_ffn, w_up, conv_w, conv_b, w_down):
    params = dict(g_mix=g_mix, w_in=w_in, q_norm=q_norm, k_norm=k_norm, lambda_q1=lambda_q1,
                  lambda_k1=lambda_k1, lambda_q2=lambda_q2, lambda_k2=lambda_k2, attn_subln=attn_subln,
                  mu_prev=mu_prev, mu_next=mu_next, decay_w0=decay_w0, decay_w2=decay_w2, aaa_a0=aaa_a0,
                  aaa_a2=aaa_a2, gate_g2=gate_g2, k_k=k_k, k_a=k_a, r_k=r_k, lnx_w=lnx_w, lnx_b=lnx_b,
                  g_memnorm=g_memnorm, w_mem_kv=w_mem_kv, mq_norm=mq_norm, mk_norm=mk_norm, p_attn=p_attn,
                  p_rwkv=p_rwkv, p_mem=p_mem, w_o=w_o, g_ffn=g_ffn, w_up=w_up, conv_w=conv_w, conv_b=conv_b,
                  w_down=w_down)
    layers = [{name: arr[l] for name, arr in params.items()} for l in range(g_mix.shape[0])]
    weights = [_prepare_weights(p) for p in layers]
    outs = []
    for x, mem in ((x_prompt, mem_prompt), (x_sample, mem_sample)):
        for l, (p, w) in enumerate(zip(layers, weights)):
            x = _layer(x, mem, p, w, l)
        outs.append(x)
    return tuple(outs)
```

```python
import functools
import math

import jax
import jax.numpy as jnp
from jax import lax
from jax.experimental import pallas as pl
from jax.experimental.pallas import tpu as pltpu

F32 = jnp.float32
BF16 = jnp.bfloat16

A_HEADS = 8
A_HEAD_DIM = 64
A_WIDTH = 1024
ROPE_THETA = 10000.0
R_HEAD = 64
R_WIDTH = 1024
R_COLS = 3488
R_COLS_PAD = 3584
GATE_RANK = 160
GATE_RANK_PAD = 256
LNX_EPS = 64e-5
N_MEM = 256
M_HEADS = 4
M_HEAD_DIM = 256
M_WIDTH = 1024
D_FF = 5632
EPS = 1e-6
LANES = 128
CHUNK = 64
INV_BASE_BLOCK = 4
SCAN_CHUNKS_PER_STEP = 4
LOG2E = 1.4426950408889634

VMEM_ROWWISE_MIB = 40
VMEM_TILE_MIB = 48
VMEM_WIDE_MIB = 56
VMEM_FUSED_MIB = 58

NN = (((1,), (0,)), ((), ()))
NT = (((1,), (1,)), ((), ()))
TN = (((0,), (0,)), ((), ()))


def _params(sem, vmem_mb):
    return pltpu.CompilerParams(dimension_semantics=sem, vmem_limit_bytes=vmem_mb << 20)


def _dot(a, b, dims=NN):
    return lax.dot_general(a, b, dims, preferred_element_type=F32)


def _rmsnorm_kernel(x_ref, g_ref, o_ref):
    x = x_ref[...]
    ms = jnp.mean(x * x, axis=-1, keepdims=True)
    o_ref[...] = (x * lax.rsqrt(ms + EPS) * g_ref[...]).astype(o_ref.dtype)


def rmsnorm_rows(x2, g, bm=512):
    m, d = x2.shape
    bm = min(bm, m)
    return pl.pallas_call(
        _rmsnorm_kernel,
        out_shape=jax.ShapeDtypeStruct((m, d), BF16),
        grid=(m // bm,),
        in_specs=[pl.BlockSpec((bm, d), lambda i: (i, 0)),
                  pl.BlockSpec((1, d), lambda i: (0, 0))],
        out_specs=pl.BlockSpec((bm, d), lambda i: (i, 0)),
        compiler_params=_params(("parallel",), VMEM_ROWWISE_MIB),
        name="rmsnorm_rows",
    )(x2, g.reshape(1, d))


def _mm_kernel(a_ref, b_ref, o_ref):
    o_ref[...] = _dot(a_ref[...], b_ref[...]).astype(o_ref.dtype)


def matmul(a, b, out_dtype, bm=1024, bn=1024):
    m, k = a.shape
    n = b.shape[1]
    bm, bn = min(bm, m), min(bn, n)
    return pl.pallas_call(
        _mm_kernel,
        out_shape=jax.ShapeDtypeStruct((m, n), out_dtype),
        grid=(m // bm, n // bn),
        in_specs=[pl.BlockSpec((bm, k), lambda i, j: (i, 0)),
                  pl.BlockSpec((k, bn), lambda i, j: (0, j))],
        out_specs=pl.BlockSpec((bm, bn), lambda i, j: (i, j)),
        compiler_params=_params(("parallel", "arbitrary"), VMEM_TILE_MIB),
        name="matmul",
    )(a, b)


def _mm_res_kernel(a_ref, b_ref, r_ref, o_ref):
    o_ref[...] = r_ref[...] + _dot(a_ref[...], b_ref[...])


def matmul_residual(a, b, res, bm=1024, bn=512):
    m, k = a.shape
    n = b.shape[1]
    bm, bn = min(bm, m), min(bn, n)
    return pl.pallas_call(
        _mm_res_kernel,
        out_shape=jax.ShapeDtypeStruct((m, n), F32),
        grid=(m // bm, n // bn),
        in_specs=[pl.BlockSpec((bm, k), lambda i, j: (i, 0)),
                  pl.BlockSpec((k, bn), lambda i, j: (0, j)),
                  pl.BlockSpec((bm, bn), lambda i, j: (i, j))],
        out_specs=pl.BlockSpec((bm, bn), lambda i, j: (i, j)),
        compiler_params=_params(("parallel", "arbitrary"), VMEM_WIDE_MIB),
        name="matmul_residual",
    )(a, b, res)


def _mm_res_norm_kernel(a_ref, b_ref, r_ref, g_ref, x_ref, h_ref):
    x = r_ref[...] + _dot(a_ref[...], b_ref[...])
    x_ref[...] = x
    ms = jnp.mean(x * x, axis=-1, keepdims=True)
    h_ref[...] = (x * lax.rsqrt(ms + EPS) * g_ref[...]).astype(h_ref.dtype)


def matmul_residual_norm(a, b, res, g, bm=512):
    m, k = a.shape
    n = b.shape[1]
    bm = min(bm, m)
    return pl.pallas_call(
        _mm_res_norm_kernel,
        out_shape=(jax.ShapeDtypeStruct((m, n), F32), jax.ShapeDtypeStruct((m, n), BF16)),
        grid=(m // bm,),
        in_specs=[pl.BlockSpec((bm, k), lambda i: (i, 0)),
                  pl.BlockSpec((k, n), lambda i: (0, 0)),
                  pl.BlockSpec((bm, n), lambda i: (i, 0)),
                  pl.BlockSpec((1, n), lambda i: (0, 0))],
        out_specs=(pl.BlockSpec((bm, n), lambda i: (i, 0)), pl.BlockSpec((bm, n), lambda i: (i, 0))),
        compiler_params=_params(("parallel",), VMEM_WIDE_MIB),
        name="matmul_residual_norm",
    )(a, b, res, g.reshape(1, n))


def _group_ones(group):
    r = lax.broadcasted_iota(jnp.int32, (2 * LANES, LANES), 0)
    c = lax.broadcasted_iota(jnp.int32, (2 * LANES, LANES), 1)
    return jnp.where((r % LANES) // group == c // group, 1.0, 0.0).astype(BF16)


def _group_sum(x, ones):
    hi = x.astype(BF16)
    lo = (x - hi.astype(F32)).astype(BF16)
    outs = []
    for c in range(x.shape[1] // LANES):
        sl = slice(LANES * c, LANES * (c + 1))
        outs.append(_dot(jnp.concatenate([hi[:, sl], lo[:, sl]], axis=1), ones))
    return outs[0] if len(outs) == 1 else jnp.concatenate(outs, axis=1)


def _tile_lanes(x, reps):
    return x if reps == 1 else jnp.concatenate([x] * reps, axis=1)


def _shift_rows(u, prev_row, next_row):
    n = u.shape[0]
    row = lax.broadcasted_iota(jnp.int32, u.shape, 0)
    up = jnp.where(row == 0, prev_row, pltpu.roll(u, 1, 0))
    dn = jnp.where(row == n - 1, next_row, pltpu.roll(u, n - 1, 0))
    return up, dn


def _norm_rope(x, g, cos, sin, scale):
    reps = x.shape[1] // LANES
    ss = _group_sum(x * x, _group_ones(A_HEAD_DIM))
    xn = x * lax.rsqrt(ss * (1.0 / A_HEAD_DIM) + EPS) * _tile_lanes(g, reps)
    lane = lax.broadcasted_iota(jnp.int32, x.shape, 1)
    half = A_HEAD_DIM // 2
    rot = jnp.where(lane % A_HEAD_DIM < half,
                    -pltpu.roll(xn, x.shape[1] - half, 1), pltpu.roll(xn, half, 1))
    return (xn * _tile_lanes(cos, reps) + rot * _tile_lanes(sin, reps)) * scale


def _mm_qkprep_kernel(a_ref, b_ref, xq_ref, xk_ref, cos_ref, sin_ref, gq_ref, gk_ref, z_ref, qk_ref):
    z_ref[...] = _dot(a_ref[...], b_ref[...]).astype(z_ref.dtype)
    cos, sin = cos_ref[...], sin_ref[...]
    q = _norm_rope(xq_ref[...].astype(F32), gq_ref[...], cos, sin, A_HEAD_DIM ** -0.5 * LOG2E)
    k = _norm_rope(xk_ref[...].astype(F32), gk_ref[...], cos, sin, 1.0)
    qk_ref[:, :A_WIDTH] = q.astype(qk_ref.dtype)
    qk_ref[:, A_WIDTH:] = k.astype(qk_ref.dtype)


def matmul_with_qk_prep(h, w_r, zqkv, t, q_norm, k_norm, bm=1024, nsub=4):
    m, k = h.shape
    n = w_r.shape[1]
    bm = min(bm, m)
    bn, sub = n // nsub, bm // nsub
    nt = t // sub
    cos, sin = _rope_tables(t)
    gq = jnp.tile(q_norm.reshape(1, A_HEAD_DIM), (1, LANES // A_HEAD_DIM))
    gk = jnp.tile(k_norm.reshape(1, A_HEAD_DIM), (1, LANES // A_HEAD_DIM))
    slab = lambda i, j: i * nsub + j
    return pl.pallas_call(
        _mm_qkprep_kernel,
        out_shape=(jax.ShapeDtypeStruct((m, n), BF16), jax.ShapeDtypeStruct((m, 2 * A_WIDTH), BF16)),
        grid=(m // bm, nsub),
        in_specs=[pl.BlockSpec((bm, k), lambda i, j: (i, 0)),
                  pl.BlockSpec((k, bn), lambda i, j: (0, j)),
                  pl.BlockSpec((sub, A_WIDTH), lambda i, j: (slab(i, j), 0)),
                  pl.BlockSpec((sub, A_WIDTH), lambda i, j: (slab(i, j), 1)),
                  pl.BlockSpec((sub, LANES), lambda i, j: (slab(i, j) % nt, 0)),
                  pl.BlockSpec((sub, LANES), lambda i, j: (slab(i, j) % nt, 0)),
                  pl.BlockSpec((1, LANES), lambda i, j: (0, 0)),
                  pl.BlockSpec((1, LANES), lambda i, j: (0, 0))],
        out_specs=(pl.BlockSpec((bm, bn), lambda i, j: (i, j)),
                   pl.BlockSpec((sub, 2 * A_WIDTH), lambda i, j: (slab(i, j), 0))),
        compiler_params=_params(("parallel", "arbitrary"), VMEM_WIDE_MIB),
        name="matmul_with_qk_prep",
    )(h, w_r, zqkv, zqkv, cos, sin, gq, gk)


def _dattn_kernel(lq1_ref, lk1_ref, lq2_ref, lk2_ref, g_ref, q_ref, k_ref, v_ref, o_ref,
                  m_ref, acc_ref, s_ref, qq_ref, *, bq, bk, group, lam_init):
    t = k_ref.shape[1]
    nk, nq = t // bk, t // bq
    ones = jnp.ones((bk, LANES), BF16)
    lane = lax.broadcasted_iota(jnp.int32, (bq, LANES), 1)
    first = lane < A_HEAD_DIM
    lam = (jnp.exp(jnp.sum(lq1_ref[...] * lk1_ref[...], keepdims=True))
           - jnp.exp(jnp.sum(lq2_ref[...] * lk2_ref[...], keepdims=True)) + lam_init)

    def stack_q(qi, slot):
        start = pl.multiple_of(qi * bq, bq)
        q = q_ref[0, pl.ds(start, bq), :]
        zero = jnp.zeros_like(q)
        qq_ref[slot] = jnp.concatenate([jnp.where(first, q, zero), jnp.where(first, zero, q)], axis=0)

    def scores(slot, j):
        start = pl.multiple_of(j * bk, bk)
        return _dot(qq_ref[slot], k_ref[0, pl.ds(start, bk), :], NT)

    def accumulate(j, s):
        start = pl.multiple_of(j * bk, bk)
        vs = v_ref[0, pl.ds(start, bk), :]
        m_prev = m_ref[...]
        m_new = jnp.maximum(m_prev, jnp.max(s, axis=-1, keepdims=True))
        alpha = jnp.exp2(m_prev - m_new)
        p = jnp.exp2(s - _tile_lanes(m_new, bk // LANES))
        pv = _dot(p.astype(BF16), jnp.concatenate([vs, ones], axis=1))
        acc_ref[...] = _tile_lanes(alpha, 2) * acc_ref[...] + pv
        m_ref[...] = m_new

    def run_group(slot, j0, next_scores):
        for u in range(group):
            if u < group - 1:
                s_ref[(u + 1) % 2] = scores(slot, j0 + u + 1)
            else:
                next_scores()
            accumulate(j0 + u, s_ref[u % 2])

    stack_q(0, 0)
    s_ref[0] = scores(0, 0)

    def q_block(qi, carry):
        slot = qi % 2
        m_ref[...] = jnp.full(m_ref.shape, -jnp.inf, F32)
        acc_ref[...] = jnp.zeros(acc_ref.shape, F32)

        def same_q(j0):
            def issue():
                s_ref[0] = scores(slot, j0 + group)
            return issue

        def body(i, c):
            run_group(slot, group * i, same_q(group * i))
            return c

        lax.fori_loop(0, nk // group - 1, body, 0)

        def next_q():
            stack_q(jnp.minimum(qi + 1, nq - 1), 1 - slot)
            s_ref[0] = scores(1 - slot, 0)

        run_group(slot, nk - group, next_q)
        o12 = acc_ref[:, :LANES] / acc_ref[:, LANES:]
        o = o12[:bq] - lam * o12[bq:]
        ms = jnp.mean(o * o, axis=-1, keepdims=True)
        y = (o * lax.rsqrt(ms + EPS) * g_ref[...]) * (1.0 - lam_init)
        o_ref[0, pl.ds(pl.multiple_of(qi * bq, bq), bq), :] = y.astype(o_ref.dtype)
        return carry

    lax.fori_loop(0, nq, q_block, 0)


def diff_attention(qk, zqkv, lq1, lk1, lq2, lk2, sub_g, lam_init, bq=512, bk=512):
    b, t, _ = qk.shape
    group = 8
    bq, bk = min(bq, t), min(bk, t // group)
    assert (t // bk) % group == 0
    vec = lambda a: a.reshape(1, -1)
    small = lambda n: pl.BlockSpec((1, n), lambda i, h: (0, 0))
    seq = lambda off: pl.BlockSpec((1, t, LANES), lambda i, h: (i, 0, off + h))
    return pl.pallas_call(
        functools.partial(_dattn_kernel, bq=bq, bk=bk, group=group, lam_init=lam_init),
        out_shape=jax.ShapeDtypeStruct((b, t, A_WIDTH), BF16),
        grid=(b, A_HEADS),
        in_specs=[small(A_HEAD_DIM), small(A_HEAD_DIM), small(A_HEAD_DIM), small(A_HEAD_DIM),
                  small(LANES), seq(0), seq(A_HEADS), seq(2 * A_HEADS)],
        out_specs=seq(0),
        scratch_shapes=[pltpu.VMEM((2 * bq, LANES), F32), pltpu.VMEM((2 * bq, 2 * LANES), F32),
                        pltpu.VMEM((2, 2 * bq, bk), F32), pltpu.VMEM((2, 2 * bq, LANES), BF16)],
        compiler_params=_params(("parallel", "arbitrary"), VMEM_WIDE_MIB),
        name="diff_attention",
    )(vec(lq1), vec(lk1), vec(lq2), vec(lk2), vec(sub_g), qk, qk, zqkv)


def _mem_attn_kernel(q_ref, kv_ref, gq_ref, gk_ref, o_ref):
    outs = []
    for h in range(M_HEADS):
        sl = slice(h * M_HEAD_DIM, (h + 1) * M_HEAD_DIM)
        qh = q_ref[0, :, sl].astype(F32)
        qn = qh * lax.rsqrt(jnp.mean(qh * qh, axis=-1, keepdims=True) + EPS) * gq_ref[...]
        kh = kv_ref[0, :, sl]
        kn = kh * lax.rsqrt(jnp.mean(kh * kh, axis=-1, keepdims=True) + EPS) * gk_ref[...]
        vh = kv_ref[0, :, M_WIDTH + h * M_HEAD_DIM:M_WIDTH + (h + 1) * M_HEAD_DIM]
        s = _dot((qn * M_HEAD_DIM ** -0.5).astype(BF16), kn.astype(BF16), NT)
        p = jnp.exp(s - jnp.max(s, axis=-1, keepdims=True))
        o = _dot(p.astype(BF16), vh.astype(BF16)) / jnp.sum(p, axis=-1, keepdims=True)
        outs.append(o)
    o_ref[0] = jnp.concatenate(outs, axis=1).astype(o_ref.dtype)


def memory_attention(zm, kv, mq_norm, mk_norm, bt=512):
    b, t, _ = zm.shape
    bt = min(bt, t)
    return pl.pallas_call(
        _mem_attn_kernel,
        out_shape=jax.ShapeDtypeStruct((b, t, M_WIDTH), BF16),
        grid=(b, t // bt),
        in_specs=[pl.BlockSpec((1, bt, M_WIDTH), lambda i, j: (i, j, 0)),
                  pl.BlockSpec((1, N_MEM, 2 * M_WIDTH), lambda i, j: (i, 0, 0)),
                  pl.BlockSpec((1, M_HEAD_DIM), lambda i, j: (0, 0)),
                  pl.BlockSpec((1, M_HEAD_DIM), lambda i, j: (0, 0))],
        out_specs=pl.BlockSpec((1, bt, M_WIDTH), lambda i, j: (i, j, 0)),
        compiler_params=_params(("parallel", "arbitrary"), VMEM_ROWWISE_MIB),
        name="memory_attention",
    )(zm, kv, mq_norm.reshape(1, -1), mk_norm.reshape(1, -1))


def _mm_rwkvprep_kernel(a_ref, b_ref, z_ref, zp_ref, zn_ref, mup_ref, mun_ref, w0_ref, w2_ref, a0_ref, a2_ref,
                        g2_ref, kk_w_ref, ka_ref, rk_ref,
                        gates_ref, r_ref, v_ref, kk_ref, lw_ref, kd_ref, bb_ref, g_ref, bonus_ref, *, nt):
    n_chunks = 3
    cw = b_ref.shape[1] // n_chunks

    def gate_chunk(c):
        cols = slice(c * cw, (c + 1) * cw)
        gates_ref[:, cols] = jax.nn.sigmoid(_dot(a_ref[...], b_ref[:, cols])).astype(gates_ref.dtype)

    gate_chunk(0)
    j = (pl.program_id(0) * pl.num_programs(1) + pl.program_id(1)) % nt
    z = z_ref[...].astype(F32)
    last = zp_ref.shape[0] - 1
    prev_row = jnp.where(j > 0, zp_ref[last:last + 1, :].astype(F32), 0.0)
    next_row = jnp.where(j < nt - 1, zn_ref[0:1, :].astype(F32), 0.0)
    zp, zn = _shift_rows(z, prev_row, next_row)
    z = z + mup_ref[...] * (zp - z) + mun_ref[...] * (zn - z)
    w = R_WIDTH
    r, k, v = z[:, 0:w], z[:, w:2 * w], z[:, 2 * w:3 * w]
    zw = z[:, 3 * w:3 * w + LANES]
    za = z[:, 3 * w + LANES:3 * w + 2 * LANES]
    zg = z[:, 3 * w + 2 * LANES:]
    r_ref[...] = r.astype(r_ref.dtype)
    v_ref[...] = v.astype(v_ref.dtype)
    wlin = w0_ref[...] + _dot(jnp.tanh(zw).astype(BF16), w2_ref[...])
    alin = a0_ref[...] + _dot(za.astype(BF16), a2_ref[...])
    g = _dot(jax.nn.sigmoid(zg).astype(BF16), g2_ref[...])
    g_ref[...] = g.astype(g_ref.dtype)
    ones = _group_ones(R_HEAD)
    kk = k * kk_w_ref[...]
    kk_sq = _group_sum(kk * kk, ones)
    gate_chunk(1)
    lw = -math.exp(-0.5) * jax.nn.sigmoid(wlin)
    lw_ref[0] = lw[:, :w]
    lw_ref[1] = lw[:, w:]
    a = jax.nn.sigmoid(alin)
    kk = kk / jnp.maximum(jnp.sqrt(kk_sq), 1e-12)
    kk_ref[...] = kk.astype(kk_ref.dtype)
    bb_ref[0] = (kk * a[:, :w]).astype(bb_ref.dtype)
    bb_ref[1] = (kk * a[:, w:]).astype(bb_ref.dtype)
    gate_chunk(2)
    ka = ka_ref[...]
    kd0 = k * (1.0 + (a[:, :w] - 1.0) * ka)
    kd1 = k * (1.0 + (a[:, w:] - 1.0) * ka)
    kd_ref[0] = kd0.astype(kd_ref.dtype)
    kd_ref[1] = kd1.astype(kd_ref.dtype)
    bonus_sum = _group_sum(r * (kd0 + kd1) * rk_ref[...], ones)
    bonus_ref[...] = (bonus_sum * v).astype(bonus_ref.dtype)


def gates_with_rwkv_prep(h, w_g, zr, t, mu_prev, mu_next, decay_w0, decay_w2, aaa_a0, aaa_a2, gate_g2,
                         k_k, k_a, r_k, bm=1024, nsub=4):
    m, k = h.shape
    n = w_g.shape[1]
    cols = zr.shape[1]
    bm = min(bm, m)
    bn, sub = n // nsub, bm // nsub
    w = R_WIDTH
    pad = cols - R_COLS
    mup = jnp.pad(mu_prev, (0, pad)).reshape(1, cols)
    mun = jnp.pad(mu_next, (0, pad)).reshape(1, cols)

    def blockdiag(m):
        z = jnp.zeros_like(m[0])
        return jnp.concatenate([jnp.concatenate([m[0], z], axis=1),
                                jnp.concatenate([z, m[1]], axis=1)], axis=0).astype(BF16)

    g2 = jnp.pad(gate_g2, ((0, GATE_RANK_PAD - GATE_RANK), (0, 0))).astype(BF16)
    row = lambda n: pl.BlockSpec((1, n), lambda i, j: (0, 0))
    full = lambda r_, c_: pl.BlockSpec((r_, c_), lambda i, j: (0, 0))
    halo = 16
    hb = sub // halo
    nh = m // halo
    slab = lambda i, j: i * nsub + j
    tok = pl.BlockSpec((sub, w), lambda i, j: (slab(i, j), 0))
    tok2 = pl.BlockSpec((2, sub, w), lambda i, j: (0, slab(i, j), 0))
    s1 = jax.ShapeDtypeStruct((m, w), BF16)
    s2 = jax.ShapeDtypeStruct((2, m, w), BF16)
    lw_shape = jax.ShapeDtypeStruct((2, m, w), F32)
    return pl.pallas_call(
        functools.partial(_mm_rwkvprep_kernel, nt=t // sub),
        out_shape=(jax.ShapeDtypeStruct((m, n), BF16), s1, s1, s1, lw_shape, s2, s2, s1, s1),
        grid=(m // bm, nsub),
        in_specs=[pl.BlockSpec((bm, k), lambda i, j: (i, 0)),
                  pl.BlockSpec((k, bn), lambda i, j: (0, j)),
                  pl.BlockSpec((sub, cols), lambda i, j: (slab(i, j), 0)),
                  pl.BlockSpec((halo, cols), lambda i, j: (jnp.maximum(slab(i, j) * hb - 1, 0), 0)),
                  pl.BlockSpec((halo, cols), lambda i, j: (jnp.minimum((slab(i, j) + 1) * hb, nh - 1), 0)),
                  row(cols), row(cols), row(2 * w), full(LANES, 2 * w), row(2 * w), full(LANES, 2 * w),
                  full(GATE_RANK_PAD, w), row(w), row(w), row(w)],
        out_specs=(pl.BlockSpec((bm, bn), lambda i, j: (i, j)), tok, tok, tok, tok2, tok2, tok2, tok, tok),
        compiler_params=_params(("parallel", "arbitrary"), VMEM_FUSED_MIB),
        name="gates_with_rwkv_prep",
    )(h, w_g, zr, zr, zr, mup, mun, decay_w0.reshape(1, 2 * w), blockdiag(decay_w2), aaa_a0.reshape(1, 2 * w),
      blockdiag(aaa_a2), g2, k_k.reshape(1, w), k_a.reshape(1, w), r_k.reshape(1, w))


def _split3(x):
    hi = x.astype(BF16)
    r1 = x - hi.astype(F32)
    mid = r1.astype(BF16)
    lo = (r1 - mid.astype(F32)).astype(BF16)
    return hi, mid, lo


def _rwkv_scan_kernel(rf_ref, vf_ref, kkf_ref, lwf_ref, kdf_ref, bbf_ref,
                      rb_ref, vb_ref, kkb_ref, lwb_ref, kdb_ref, bbb_ref, yf_ref, yb_ref, s_ref):
    C = CHUNK
    n_pairs = R_WIDTH // LANES
    assert C == R_HEAD

    @pl.when(pl.program_id(1) == 0)
    def _():
        s_ref[...] = jnp.zeros(s_ref.shape, F32)

    ti = lax.broadcasted_iota(jnp.int32, (C, C), 0)
    si = lax.broadcasted_iota(jnp.int32, (C, C), 1)
    r2 = lax.broadcasted_iota(jnp.int32, (2 * C, 2 * C), 0)
    c2 = lax.broadcasted_iota(jnp.int32, (2 * C, 2 * C), 1)
    same_head = (r2 // C) == (c2 // C)
    lane = lax.broadcasted_iota(jnp.int32, (C, LANES), 1)
    row = lax.broadcasted_iota(jnp.int32, (C, LANES), 0)
    first = lane < R_HEAD
    src = lane % C
    eye_h = jnp.where(row == src, 1.0, 0.0)

    def stack(x):
        return jnp.concatenate([jnp.where(first, x, 0.0), jnp.where(first, 0.0, x)], axis=0)

    n_sub = rf_ref.shape[0] // C
    chains = []
    for d, (r_ref, v_ref, kk_ref, lw_ref, kd_ref, bb_ref, y_ref) in enumerate(
            ((rf_ref, vf_ref, kkf_ref, lwf_ref, kdf_ref, bbf_ref, yf_ref),
             (rb_ref, vb_ref, kkb_ref, lwb_ref, kdb_ref, bbb_ref, yb_ref))):
        sign = 1 if d == 0 else -1
        strict_h = sign * (row - src) > 0
        incl_h = sign * (row - src) >= 0
        tri = jnp.where(sign * (ti - si) >= 0, 1.0, 0.0).astype(BF16)
        for q in range(n_sub):
            k = q if d == 0 else n_sub - 1 - q
            rows = slice(k * C, (k + 1) * C)
            lw_all = lw_ref[0, rows, :]
            hi, mid, lo = _split3(lw_all)
            cum_all = _dot(tri, hi) + (_dot(tri, mid) + _dot(tri, lo))
            tot_all = jnp.sum(lw_all, axis=0, keepdims=True)
            for p in range(n_pairs):
                sl = slice(p * LANES, (p + 1) * LANES)
                chains.append(dict(d=d, q=q, p=p, sl=sl, rows=rows, strict_h=strict_h, incl_h=incl_h,
                                   lw=lw_all[:, sl], cum=cum_all[:, sl], tot=tot_all[:, sl],
                                   r=r_ref, v=v_ref, kk=kk_ref, kd=kd_ref, bb=bb_ref, y=y_ref))

    for ch in chains:
        sl, rows, lw, cum, tot = ch["sl"], ch["rows"], ch["lw"], ch["cum"], ch["tot"]
        r, v, kk = (ch[n][rows, sl].astype(F32) for n in ("r", "v", "kk"))
        kd, bb = (ch[n][0, rows, sl].astype(F32) for n in ("kd", "bb"))
        e_neg = jnp.exp(-cum)
        e_end = jnp.exp(tot - cum)
        at = -kk * jnp.exp(cum - lw)
        rt = r * jnp.exp(cum)
        ch["ar"] = jnp.concatenate([at, rt], axis=0).astype(BF16)
        ch["bk"] = jnp.concatenate([stack(bb * e_neg), stack(kd * e_neg)], axis=0).astype(BF16)
        ch["bk_end"] = jnp.concatenate([bb * e_end, kd * e_end], axis=0).astype(BF16)
        ch["v"] = v.astype(BF16)
    for ch in chains:
        ch["pm"] = _dot(ch["ar"], ch["bk"], NT)
    for ch in chains:
        pm = ch["pm"]
        ch["lak_mrk"] = jnp.concatenate([jnp.where(ch["strict_h"], pm[:C, 2 * C:], 0.0),
                                         jnp.where(ch["incl_h"], pm[C:, 2 * C:], 0.0)],
                                        axis=0).astype(BF16)
        ch["mrb"] = jnp.where(ch["incl_h"], pm[C:, :2 * C], 0.0).astype(BF16)
        ch["lh"] = jnp.where(ch["strict_h"], pm[:C, :2 * C], 0.0)
    for ch in chains:
        ch["lv"] = _dot(ch["lak_mrk"], stack(ch["v"]))
    def product(left, right):
        return _dot(left.astype(BF16), stack(right).astype(BF16))

    blk = INV_BASE_BLOCK
    diag = row // blk == src // blk
    for ch in chains:
        ch["dg"] = jnp.where(diag, ch["lh"], 0.0)
    for ch in chains:
        ch["dg2"] = product(ch["dg"], ch["dg"])
    for ch in chains:
        ch["tm"] = (eye_h + ch["dg"]) + product(eye_h + ch["dg"], ch["dg2"])
    while blk < C:
        couple = (row // (2 * blk) == src // (2 * blk)) & (row // blk != src // blk)
        for ch in chains:
            ch["x"] = product(jnp.where(couple, ch["lh"], 0.0), ch["tm"])
        for ch in chains:
            ch["tm"] = ch["tm"] + product(ch["tm"], ch["x"])
        blk *= 2
    state = {(d, p): s_ref[d, p] for d in range(2) for p in range(n_pairs)}
    for q in range(n_sub):
        now = [ch for ch in chains if ch["q"] == q]
        for ch in now:
            ch["a_s"] = _dot(ch["ar"], state[ch["d"], ch["p"]].astype(BF16), NT)
        for ch in now:
            rhs = ch["a_s"][:C] + ch["lv"][:C]
            ch["ub"] = product(ch["tm"], rhs).astype(BF16)
        for ch in now:
            ch["y"][ch["rows"], ch["sl"]] = ch["a_s"][C:] + ch["lv"][C:] + _dot(ch["mrb"], stack(ch["ub"]))
        for ch in now:
            uv = jnp.concatenate([ch["ub"], ch["v"]], axis=0)
            grow = jnp.where(same_head, _dot(uv, ch["bk_end"], TN), 0.0)
            state[ch["d"], ch["p"]] = state[ch["d"], ch["p"]] * jnp.exp(ch["tot"]) + grow
    for (d, p), s in state.items():
        s_ref[d, p] = s


def rwkv_scan(r, v, kk, lw, kd, bb, t):
    m, w = r.shape
    rows = SCAN_CHUNKS_PER_STEP * CHUNK
    nc = t // rows
    fwd = pl.BlockSpec((rows, w), lambda i, c: (i * nc + c, 0))
    bwd = pl.BlockSpec((rows, w), lambda i, c: (i * nc + nc - 1 - c, 0))
    fwd2 = pl.BlockSpec((1, rows, w), lambda i, c: (0, i * nc + c, 0))
    bwd2 = pl.BlockSpec((1, rows, w), lambda i, c: (1, i * nc + nc - 1 - c, 0))
    out = jax.ShapeDtypeStruct((m, w), F32)
    return pl.pallas_call(
        _rwkv_scan_kernel,
        out_shape=(out, out),
        grid=(m // t, nc),
        in_specs=[fwd, fwd, fwd, fwd2, fwd2, fwd2, bwd, bwd, bwd, bwd2, bwd2, bwd2],
        out_specs=(fwd, bwd),
        scratch_shapes=[pltpu.VMEM((2, w // LANES, LANES, LANES), F32)],
        compiler_params=_params(("parallel", "arbitrary"), VMEM_TILE_MIB),
        name="rwkv_scan",
    )(r, v, kk, lw, kd, bb, r, v, kk, lw, kd, bb)


def _rwkv_post_kernel(yf_ref, yb_ref, bonus_ref, g_ref, w_ref, b_ref, o_ref):
    y = yf_ref[...] + yb_ref[...]
    ones = _group_ones(R_HEAD)
    mu = _group_sum(y, ones) * (1.0 / R_HEAD)
    dlt = y - mu
    var = _group_sum(dlt * dlt, ones) * (1.0 / R_HEAD)
    yn = dlt * lax.rsqrt(var + LNX_EPS) * w_ref[...] + b_ref[...]
    o_ref[...] = ((yn + bonus_ref[...].astype(F32)) * g_ref[...].astype(F32)).astype(o_ref.dtype)


def rwkv_post(y_fwd, y_bwd, bonus, g, lnx_w, lnx_b, bt=512):
    m, w = y_fwd.shape
    bt = min(bt, m)
    tok = pl.BlockSpec((bt, w), lambda i: (i, 0))
    row = pl.BlockSpec((1, w), lambda i: (0, 0))
    return pl.pallas_call(
        _rwkv_post_kernel,
        out_shape=jax.ShapeDtypeStruct((m, w), BF16),
        grid=(m // bt,),
        in_specs=[tok, tok, tok, tok, row, row],
        out_specs=tok,
        compiler_params=_params(("parallel",), VMEM_ROWWISE_MIB),
        name="rwkv_post",
    )(y_fwd, y_bwd, bonus, g, lnx_w.reshape(1, w), lnx_b.reshape(1, w))


def _merge_kernel(oa_ref, or_ref, om_ref, pa_ref, pr_ref, pm_ref, ga_ref, gr_ref, gm_ref, o_ref):
    acc = ga_ref[...].astype(F32) * _dot(oa_ref[...], pa_ref[...])
    acc += gr_ref[...].astype(F32) * _dot(or_ref[...], pr_ref[...])
    acc += gm_ref[...].astype(F32) * _dot(om_ref[...], pm_ref[...])
    o_ref[...] = acc.astype(o_ref.dtype)


def gated_merge(o_a, o_r, o_m, p_a, p_r, p_m, gates, bm=1024, bn=1024):
    m, k = o_a.shape
    n = p_a.shape[1]
    bm, bn = min(bm, m), min(bn, n)
    nb = n // bn
    a_spec = pl.BlockSpec((bm, k), lambda i, j: (i, 0))
    w_spec = pl.BlockSpec((k, bn), lambda i, j: (0, j))
    gate = lambda o: pl.BlockSpec((bm, bn), lambda i, j: (i, j + o * nb))
    return pl.pallas_call(
        _merge_kernel,
        out_shape=jax.ShapeDtypeStruct((m, n), BF16),
        grid=(m // bm, nb),
        in_specs=[a_spec, a_spec, a_spec, w_spec, w_spec, w_spec, gate(0), gate(1), gate(2)],
        out_specs=pl.BlockSpec((bm, bn), lambda i, j: (i, j)),
        compiler_params=_params(("parallel", "arbitrary"), VMEM_WIDE_MIB),
        name="gated_merge",
    )(o_a, o_r, o_m, p_a, p_r, p_m, gates, gates, gates)


def _ffn_up_kernel(h_ref, hp_ref, hn_ref, wg_ref, wv_ref, cg_ref, cv_ref, bg_ref, bv_ref, o_ref, lhs_ref):
    j = pl.program_id(1)
    nt = pl.num_programs(1)
    bm = h_ref.shape[1]
    halo = hp_ref.shape[1]

    @pl.when(pl.program_id(2) == 0)
    def _():
        lhs_ref[0:bm] = h_ref[0]
        lhs_ref[bm:bm + halo] = hp_ref[0]
        lhs_ref[bm + halo:] = hn_ref[0]

    def branch(w_ref, c_ref, b_ref):
        u_all = _dot(lhs_ref[...], w_ref[...])
        u = u_all[:bm]
        prev_row = jnp.where(j > 0, u_all[bm + halo - 1:bm + halo], 0.0)
        next_row = jnp.where(j < nt - 1, u_all[bm + halo:bm + halo + 1], 0.0)
        up, dn = _shift_rows(u, prev_row, next_row)
        c = c_ref[...]
        return c[0:1] * up + c[1:2] * u + c[2:3] * dn + b_ref[...]

    gt = branch(wg_ref, cg_ref, bg_ref)
    val = branch(wv_ref, cv_ref, bv_ref)
    o_ref[0] = (gt * jax.nn.sigmoid(gt) * val).astype(o_ref.dtype)


def ffn_up(h, w_up, conv_w, conv_b, bm=1024, bn=512):
    b, t, k = h.shape
    bm = min(bm, t)
    nb = D_FF // bn
    halo = 16
    hb = bm // halo
    nh = t // halo
    return pl.pallas_call(
        _ffn_up_kernel,
        out_shape=jax.ShapeDtypeStruct((b, t, D_FF), BF16),
        grid=(b, t // bm, nb),
        in_specs=[pl.BlockSpec((1, bm, k), lambda i, j, n: (i, j, 0)),
                  pl.BlockSpec((1, halo, k), lambda i, j, n: (i, jnp.maximum(j * hb - 1, 0), 0)),
                  pl.BlockSpec((1, halo, k), lambda i, j, n: (i, jnp.minimum((j + 1) * hb, nh - 1), 0)),
                  pl.BlockSpec((k, bn), lambda i, j, n: (0, n)),
                  pl.BlockSpec((k, bn), lambda i, j, n: (0, n + nb)),
                  pl.BlockSpec((3, bn), lambda i, j, n: (0, n)),
                  pl.BlockSpec((3, bn), lambda i, j, n: (0, n + nb)),
                  pl.BlockSpec((1, bn), lambda i, j, n: (0, n)),
                  pl.BlockSpec((1, bn), lambda i, j, n: (0, n + nb))],
        out_specs=pl.BlockSpec((1, bm, bn), lambda i, j, n: (i, j, n)),
        scratch_shapes=[pltpu.VMEM((bm + 2 * halo, k), BF16)],
        compiler_params=_params(("parallel", "parallel", "arbitrary"), VMEM_WIDE_MIB),
        name="ffn_up",
    )(h, h, h, w_up, w_up, conv_w, conv_w, conv_b.reshape(1, -1), conv_b.reshape(1, -1))


def _rope_tables(t):
    half = A_HEAD_DIM // 2
    inv = jnp.power(ROPE_THETA, -jnp.arange(half, dtype=F32) / half)
    ang = jnp.arange(t, dtype=F32)[:, None] * inv[None, :]
    reps = LANES // half
    return jnp.tile(jnp.cos(ang), (1, reps)), jnp.tile(jnp.sin(ang), (1, reps))


def _layer(x, mem, p, w, layer_idx):
    b, t, d = x.shape
    m = b * t
    lam_init = 0.8 - 0.6 * math.exp(-0.3 * layer_idx)
    x2 = x.reshape(m, d)
    h = rmsnorm_rows(x2, p["g_mix"])
    zm = matmul(h, w["w_m"], BF16).reshape(b, t, -1)
    zqkv = matmul(h, w["w_qkv"], BF16)
    zr, qk = matmul_with_qk_prep(h, w["w_r"], zqkv, t, p["q_norm"], p["k_norm"])
    gates, r, v, kk, lw, kd, bb, g, bonus = gates_with_rwkv_prep(
        h, w["w_g"], zr, t, p["mu_prev"], p["mu_next"], p["decay_w0"], p["decay_w2"], p["aaa_a0"],
        p["aaa_a2"], p["gate_g2"], p["k_k"], p["k_a"], p["r_k"])
    y_fwd, y_bwd = rwkv_scan(r, v, kk, lw, kd, bb, t)
    o_r = rwkv_post(y_fwd, y_bwd, bonus, g, p["lnx_w"], p["lnx_b"])
    hm = rmsnorm_rows(mem.reshape(b * N_MEM, d), p["g_memnorm"])
    kv = matmul(hm, w["w_mem_kv"], F32).reshape(b, N_MEM, -1)
    o_m = memory_attention(zm, kv, p["mq_norm"], p["mk_norm"])
    o_a = diff_attention(qk.reshape(b, t, -1), zqkv.reshape(b, t, -1), p["lambda_q1"], p["lambda_k1"],
                         p["lambda_q2"], p["lambda_k2"], p["attn_subln"], lam_init)
    merged = gated_merge(o_a.reshape(m, -1), o_r, o_m.reshape(m, -1),
                         w["p_attn"], w["p_rwkv"], w["p_mem"], gates)
    x2, h2 = matmul_residual_norm(merged, w["w_o"], x2, p["g_ffn"])
    act = ffn_up(h2.reshape(b, t, d), w["w_up"], p["conv_w"], p["conv_b"])
    x2 = matmul_residual(act.reshape(m, -1), w["w_down"], x2)
    return x2.reshape(b, t, d)


def _prepare_weights(p):
    w_in = p["w_in"]
    c0 = 3 * A_WIDTH
    c1 = c0 + R_COLS
    c2 = c1 + M_WIDTH
    return {
        "w_qkv": w_in[:, :c0].astype(BF16),
        "w_r": jnp.pad(w_in[:, c0:c1], ((0, 0), (0, R_COLS_PAD - R_COLS))).astype(BF16),
        "w_m": w_in[:, c1:c2].astype(BF16),
        "w_g": w_in[:, c2:].astype(BF16),
        "w_mem_kv": p["w_mem_kv"].astype(BF16),
        "p_attn": p["p_attn"].astype(BF16),
        "p_rwkv": p["p_rwkv"].astype(BF16),
        "p_mem": p["p_mem"].astype(BF16),
        "w_o": p["w_o"].astype(BF16),
        "w_up": p["w_up"].astype(BF16),
        "w_down": p["w_down"].astype(BF16),
    }


def kernel(x_prompt, x_sample, mem_prompt, mem_sample, g_mix, w_in, q_norm, k_norm, lambda_q1, lambda_k1, lambda_q2, lambda_k2, attn_subln, mu_prev, mu_next, decay_w0, decay_w2, aaa_a0, aaa_a2, gate_g2, k_k, k_a, r_k, lnx_w, lnx_b, g_memnorm, w_mem_kv, mq_norm, mk_norm, p_attn, p_rwkv, p_mem, w_o, g_ffn, w_up, conv_w, conv_b, w_down):
    params = dict(g_mix=g_mix, w_in=w_in, q_norm=q_norm, k_norm=k_norm, lambda_q1=lambda_q1,
                  lambda_k1=lambda_k1, lambda_q2=lambda_q2, lambda_k2=lambda_k2, attn_subln=attn_subln,
                  mu_prev=mu_prev, mu_next=mu_next, decay_w0=decay_w0, decay_w2=decay_w2, aaa_a0=aaa_a0,
                  aaa_a2=aaa_a2, gate_g2=gate_g2, k_k=k_k, k_a=k_a, r_k=r_k, lnx_w=lnx_w, lnx_b=lnx_b,
                  g_memnorm=g_memnorm, w_mem_kv=w_mem_kv, mq_norm=mq_norm, mk_norm=mk_norm, p_attn=p_attn,
                  p_rwkv=p_rwkv, p_mem=p_mem, w_o=w_o, g_ffn=g_ffn, w_up=w_up, conv_w=conv_w, conv_b=conv_b,
                  w_down=w_down)
    layers = [{name: arr[l] for name, arr in params.items()} for l in range(g_mix.shape[0])]
    weights = [_prepare_weights(p) for p in layers]
    outs = []
    for x, mem in ((x_prompt, mem_prompt), (x_sample, mem_sample)):
        for l, (p, w) in enumerate(zip(layers, weights)):
            x = _layer(x, mem, p, w, l)
        outs.append(x)
    return tuple(outs)
```
